```python
import math
import jax, jax.numpy as jnp
from jax import lax
import numpy as np


D_MODEL = 1024
BATCH = 4
SEQ = 8192
DEPTH = 2

ATT_HEADS = 4
ATT_DIM = 64
ATT_QK = ATT_HEADS * 2 * ATT_DIM
ATT_WIDTH = ATT_HEADS * 2 * ATT_DIM
ROT_DIM = ATT_DIM // 4
ROPE_THETA = 500000.0
Q_BLOCK = 128
GDN_HEADS = 4
GDN_DK = 128
GDN_DV = 128
GDN_WIDTH = GDN_HEADS * GDN_DV
GDN_CONV_CH = 2 * GDN_HEADS * GDN_DK + GDN_WIDTH
CONV_K = 4
GDN_CHUNK = 64
IN_COLS = 2 * ATT_QK + ATT_WIDTH + GDN_CONV_CH + GDN_WIDTH + 2 * GDN_HEADS + 2 * D_MODEL
D_FF = 2816
N_EXPERTS = 8
TOP_K = 2
D_EXPERT = 3584
MOE_BLOCK = 256
NORM_EPS = 1e-6
MAX_POS_OFFSET = 4096

kernel_name = 'hybrid_diffattn_gdn_moe_block'


def rms_norm(x, w):
    xf = x.astype(jnp.float32)
    y = xf * lax.rsqrt(jnp.mean(xf * xf, axis=-1, keepdims=True) + NORM_EPS)
    return (y * w.astype(jnp.float32)).astype(x.dtype)


def l2_normalize(x):
    return x * lax.rsqrt(jnp.sum(x * x, axis=-1, keepdims=True) + NORM_EPS)


def rope_tables(positions):
    inv_freq = ROPE_THETA ** (-jnp.arange(0, ROT_DIM, 2, dtype=jnp.float32) / ROT_DIM)
    ang = positions.astype(jnp.float32)[..., None] * inv_freq
    return jnp.cos(ang)[:, :, None, None, :], jnp.sin(ang)[:, :, None, None, :]


def partial_rope(x, cos, sin):
    half = ROT_DIM // 2
    x1, x2, rest = x[..., :half], x[..., half:ROT_DIM], x[..., ROT_DIM:]
    c, s = cos.astype(x.dtype), sin.astype(x.dtype)
    return jnp.concatenate([x1 * c - x2 * s, x2 * c + x1 * s, rest], axis=-1)


def diff_attention(q, k, v, lam):
    b_, s_, h_ = q.shape[:3]
    nqb = s_ // Q_BLOCK
    qb = jnp.moveaxis(q.reshape(b_, nqb, Q_BLOCK, h_, 2, ATT_DIM), 1, 0)
    key_pos = jnp.arange(s_)
    scale = ATT_DIM ** -0.5

    def block(args):
        q_blk, i = args
        s = jnp.einsum('bqhmd,bkhmd->bhmqk', q_blk, k).astype(jnp.float32) * scale
        q_pos = i * Q_BLOCK + jnp.arange(Q_BLOCK)
        causal = key_pos[None, :] <= q_pos[:, None]
        p = jax.nn.softmax(jnp.where(causal, s, -jnp.inf), axis=-1)
        a = p[:, :, 0] - lam * p[:, :, 1]
        return jnp.einsum('bhqk,bkhe->bqhe', a.astype(v.dtype), v)

    o = lax.map(block, (qb, jnp.arange(nqb)))
    return jnp.moveaxis(o, 0, 1).reshape(b_, s_, h_, 2 * ATT_DIM)


def causal_depthwise_conv(x, w):
    kw, ch = w.shape
    return lax.conv_general_dilated(x, w[:, None, :].astype(x.dtype), window_strides=(1,),
                                    padding=[(kw - 1, 0)], dimension_numbers=('NWC', 'WIO', 'NWC'),
                                    feature_group_count=ch)


def gated_delta_rule_chunked(q, k, v, g, beta):
    b_, s_, h_, dk = q.shape
    dv = v.shape[-1]
    c = GDN_CHUNK
    nc = s_ // c

    def to_chunks(t):
        return jnp.moveaxis(t.reshape((b_, nc, c) + t.shape[2:]), 3, 1)

    q, k, v, g, beta = (to_chunks(t) for t in (q, k, v, g, beta))
    gc = jnp.cumsum(g, axis=-1)
    idx = jnp.arange(c)
    incl = idx[:, None] >= idx[None, :]
    strict = idx[:, None] > idx[None, :]
    decay = jnp.exp(jnp.where(incl, gc[..., :, None] - gc[..., None, :], -jnp.inf))
    kb = k * beta[..., None]
    lmat = jnp.where(strict, jnp.einsum('bhncd,bhnjd->bhncj', kb, k) * decay, 0.0)
    rhs = jnp.concatenate([v * beta[..., None], kb * jnp.exp(gc)[..., None]], axis=-1)
    sol = lax.linalg.triangular_solve(lmat + jnp.eye(c, dtype=lmat.dtype), rhs, left_side=True,
                                      lower=True, unit_diagonal=True)
    u, w = sol[..., :dv], sol[..., dv:]
    a_qk = jnp.where(incl, jnp.einsum('bhncd,bhnjd->bhncj', q, k) * decay, 0.0)
    q_dec = q * jnp.exp(gc)[..., None]
    k_dec = k * jnp.exp(gc[..., -1:] - gc)[..., None]
    g_last = jnp.exp(gc[..., -1])
    xs = tuple(jnp.moveaxis(t, 2, 0) for t in (u, w, q_dec, k_dec, a_qk, g_last))

    def step(state, inp):
        u_c, w_c, qd, kd, aqk, gl = inp
        v_new = u_c - jnp.einsum('bhcd,bhde->bhce', w_c, state)
        o = jnp.einsum('bhcd,bhde->bhce', qd, state) + jnp.einsum('bhcj,bhje->bhce', aqk, v_new)
        state = state * gl[..., None, None] + jnp.einsum('bhcd,bhce->bhde', kd, v_new)
        return state, o

    state0 = jnp.zeros((b_, h_, dk, dv), jnp.float32)
    _, o = lax.scan(step, state0, xs)
    return jnp.transpose(o, (1, 0, 3, 2, 4)).reshape(b_, s_, h_, dv)


def hybrid_mixer(u, cos, sin, w_in, q_norm_w, k_norm_w, lam_vec, subln_w, conv_w, a_log, dt_bias,
                 gdn_norm_w, w_branch_a, w_branch_b, w_out, lam_init):
    b_, s_, _ = u.shape
    proj = u @ w_in
    widths = (ATT_QK, ATT_QK, ATT_WIDTH, GDN_CONV_CH, GDN_WIDTH, GDN_HEADS, GDN_HEADS, D_MODEL, D_MODEL)
    points, acc = [], 0
    for wd in widths[:-1]:
        acc += wd
        points.append(acc)
    aq, ak, av, bqkv, bz, bbeta, ba, gate_a, gate_b = jnp.split(proj, points, axis=-1)

    aq = partial_rope(rms_norm(aq.reshape(b_, s_, ATT_HEADS, 2, ATT_DIM), q_norm_w), cos, sin)
    ak = partial_rope(rms_norm(ak.reshape(b_, s_, ATT_HEADS, 2, ATT_DIM), k_norm_w), cos, sin)
    av = av.reshape(b_, s_, ATT_HEADS, 2 * ATT_DIM)
    lv = lam_vec.astype(jnp.float32)
    lam = jnp.exp(jnp.sum(lv[0] * lv[1])) - jnp.exp(jnp.sum(lv[2] * lv[3])) + lam_init
    oa = rms_norm(diff_attention(aq, ak, av, lam), subln_w) * (1.0 - lam_init)
    ya = oa.reshape(b_, s_, ATT_WIDTH) @ w_branch_a

    bqkv = jax.nn.silu(causal_depthwise_conv(bqkv, conv_w)).astype(jnp.float32)
    bq, bk, bv = jnp.split(bqkv, [GDN_HEADS * GDN_DK, 2 * GDN_HEADS * GDN_DK], axis=-1)
    bq = l2_normalize(bq.reshape(b_, s_, GDN_HEADS, GDN_DK)) * (GDN_DK ** -0.5)
    bk = l2_normalize(bk.reshape(b_, s_, GDN_HEADS, GDN_DK))
    bv = bv.reshape(b_, s_, GDN_HEADS, GDN_DV)
    beta = jax.nn.sigmoid(bbeta.astype(jnp.float32))
    g = -jnp.exp(a_log.astype(jnp.float32)) * jax.nn.softplus(ba.astype(jnp.float32) + dt_bias.astype(jnp.float32))
    ob = gated_delta_rule_chunked(bq, bk, bv, g, beta).astype(u.dtype)
    ob = rms_norm(ob, gdn_norm_w) * jax.nn.silu(bz.reshape(b_, s_, GDN_HEADS, GDN_DV))
    yb = ob.reshape(b_, s_, GDN_WIDTH) @ w_branch_b

    merged = jax.nn.sigmoid(gate_a) * ya + jax.nn.sigmoid(gate_b) * yb
    return merged @ w_out


def swiglu(h, w_gate_up, w_down):
    gt, up = jnp.split(h @ w_gate_up, 2, axis=-1)
    return (jax.nn.silu(gt) * up) @ w_down


def moe_swiglu(x, router_w, router_b, w_gate_up, w_down):
    b_, s_, d = x.shape
    t = b_ * s_
    xt = x.reshape(t, d)
    logits = xt.astype(jnp.float32) @ router_w.astype(jnp.float32) + router_b.astype(jnp.float32)
    top_val, top_idx = lax.top_k(logits, TOP_K)
    gates = jax.nn.softmax(top_val, axis=-1)
    n_assign = t * TOP_K
    e_flat = top_idx.reshape(n_assign)
    tok_flat = jnp.repeat(jnp.arange(t, dtype=jnp.int32), TOP_K)
    g_flat = gates.reshape(n_assign)
    order = jnp.argsort(e_flat)
    e_sorted, tok_sorted, g_sorted = e_flat[order], tok_flat[order], g_flat[order]
    counts = jnp.zeros((N_EXPERTS,), jnp.int32).at[e_flat].add(1)
    starts = jnp.cumsum(counts) - counts
    padded = (counts + MOE_BLOCK - 1) // MOE_BLOCK * MOE_BLOCK
    pad_end = jnp.cumsum(padded)
    pad_start = pad_end - padded
    dest = pad_start[e_sorted] + (jnp.arange(n_assign, dtype=jnp.int32) - starts[e_sorted])
    n_blocks = -(-n_assign // MOE_BLOCK) + N_EXPERTS
    p_len = n_blocks * MOE_BLOCK
    buf_tok = jnp.full((p_len,), t, jnp.int32).at[dest].set(tok_sorted)
    buf_g = jnp.zeros((p_len,), jnp.float32).at[dest].set(g_sorted)
    block_e = jnp.minimum(jnp.searchsorted(pad_end, jnp.arange(n_blocks) * MOE_BLOCK, side='right'),
                          N_EXPERTS - 1)
    x_pad = jnp.concatenate([xt, jnp.zeros((1, d), xt.dtype)], axis=0)
    xb = x_pad[buf_tok].reshape(n_blocks, MOE_BLOCK, d)

    def expert_block(args):
        xblk, e = args
        return swiglu(xblk, w_gate_up[e], w_down[e])

    yb = lax.map(expert_block, (xb, block_e)).reshape(p_len, d)
    y = jnp.zeros((t + 1, d), yb.dtype).at[buf_tok].add(yb * buf_g[:, None].astype(yb.dtype))
    return y[:t].reshape(b_, s_, d)


def setup_inputs(seed: int = 0) -> dict:
    key = jax.random.key(seed)
    ks = jax.random.split(key, 24)
    f32 = jnp.float32

    def nrm(k, shape, scale):
        return jax.random.normal(k, shape, f32) * scale

    n_dense = (DEPTH + 1) // 2
    n_moe = DEPTH // 2
    x = nrm(ks[0], (BATCH, SEQ, D_MODEL), 1.0)
    offsets = jax.random.randint(ks[1], (BATCH,), 0, MAX_POS_OFFSET, dtype=jnp.int32)
    positions = offsets[:, None] + jnp.arange(SEQ, dtype=jnp.int32)[None, :]
    norm_mix_w = 1.0 + nrm(ks[2], (DEPTH, D_MODEL), 0.02)
    w_in = nrm(ks[3], (DEPTH, D_MODEL, IN_COLS), D_MODEL ** -0.5)
    q_norm_w = 1.0 + nrm(ks[4], (DEPTH, ATT_DIM), 0.02)
    k_norm_w = 1.0 + nrm(ks[5], (DEPTH, ATT_DIM), 0.02)
    lam_vec = nrm(ks[6], (DEPTH, 4, ATT_DIM), 0.1)
    subln_w = 1.0 + nrm(ks[7], (DEPTH, 2 * ATT_DIM), 0.02)
    conv_w = nrm(ks[8], (DEPTH, CONV_K, GDN_CONV_CH), CONV_K ** -0.5)
    a_log = jnp.log(jax.random.uniform(ks[9], (DEPTH, GDN_HEADS), f32, 1.0, 16.0))
    dt = jnp.exp(jax.random.uniform(ks[10], (DEPTH, GDN_HEADS), f32, math.log(1e-3), math.log(1e-1)))
    dt_bias = dt + jnp.log(-jnp.expm1(-dt))
    gdn_norm_w = 1.0 + nrm(ks[11], (DEPTH, GDN_DV), 0.02)
    w_branch_a = nrm(ks[12], (DEPTH, ATT_WIDTH, D_MODEL), ATT_WIDTH ** -0.5)
    w_branch_b = nrm(ks[13], (DEPTH, GDN_WIDTH, D_MODEL), GDN_WIDTH ** -0.5)
    w_out = nrm(ks[14], (DEPTH, D_MODEL, D_MODEL), D_MODEL ** -0.5)
    norm_ffn_w = 1.0 + nrm(ks[15], (DEPTH, D_MODEL), 0.02)
    ffn_w_gate_up = nrm(ks[16], (n_dense, D_MODEL, 2 * D_FF), D_MODEL ** -0.5)
    ffn_w_down = nrm(ks[17], (n_dense, D_FF, D_MODEL), D_FF ** -0.5)
    router_w = nrm(ks[18], (n_moe, D_MODEL, N_EXPERTS), D_MODEL ** -0.5)
    router_b = nrm(ks[19], (n_moe, N_EXPERTS), 0.01)
    moe_w_gate_up = nrm(ks[20], (n_moe, N_EXPERTS, D_MODEL, 2 * D_EXPERT), D_MODEL ** -0.5)
    moe_w_down = nrm(ks[21], (n_moe, N_EXPERTS, D_EXPERT, D_MODEL), D_EXPERT ** -0.5)
    return {'x': x, 'positions': positions, 'norm_mix_w': norm_mix_w, 'w_in': w_in,
            'q_norm_w': q_norm_w, 'k_norm_w': k_norm_w, 'lam_vec': lam_vec, 'subln_w': subln_w,
            'conv_w': conv_w, 'a_log': a_log, 'dt_bias': dt_bias, 'gdn_norm_w': gdn_norm_w,
            'w_branch_a': w_branch_a, 'w_branch_b': w_branch_b, 'w_out': w_out,
            'norm_ffn_w': norm_ffn_w, 'ffn_w_gate_up': ffn_w_gate_up, 'ffn_w_down': ffn_w_down,
            'router_w': router_w, 'router_b': router_b, 'moe_w_gate_up': moe_w_gate_up,
            'moe_w_down': moe_w_down}


def reference(x, positions, norm_mix_w, w_in, q_norm_w, k_norm_w, lam_vec, subln_w, conv_w, a_log,
              dt_bias, gdn_norm_w, w_branch_a, w_branch_b, w_out, norm_ffn_w, ffn_w_gate_up,
              ffn_w_down, router_w, router_b, moe_w_gate_up, moe_w_down):
    cos, sin = rope_tables(positions)
    for layer in range(DEPTH):
        lam_init = 0.8 - 0.6 * math.exp(-0.3 * layer)
        h = rms_norm(x, norm_mix_w[layer])
        x = x + hybrid_mixer(h, cos, sin, w_in[layer], q_norm_w[layer], k_norm_w[layer], lam_vec[layer],
                             subln_w[layer], conv_w[layer], a_log[layer], dt_bias[layer],
                             gdn_norm_w[layer], w_branch_a[layer], w_branch_b[layer], w_out[layer],
                             lam_init)
        h = rms_norm(x, norm_ffn_w[layer])
        j = layer // 2
        if layer % 2 == 0:
            x = x + swiglu(h, ffn_w_gate_up[j], ffn_w_down[j])
        else:
            x = x + moe_swiglu(h, router_w[j], router_b[j], moe_w_gate_up[j], moe_w_down[j])
    return x
```

```python
import functools
import math

import jax
import jax.numpy as jnp
from jax import lax
from jax.experimental import pallas as pl
from jax.experimental.pallas import tpu as pltpu

F32 = jnp.float32
BF16 = jnp.bfloat16

ATT_HEADS = 4
ATT_DIM = 64
ROT_DIM = ATT_DIM // 4
ROPE_THETA = 500000.0
GDN_HEADS = 4
GDN_DK = 128
CONV_K = 4
N_EXPERTS = 8
NORM_EPS = 1e-6

HEAD_W = 128
GROUP_W = 512
COL_GA, COL_GB, COL_AQ, COL_AK, COL_AV, COL_BQ, COL_BK, COL_BV, COL_BZ = 0, 2, 4, 5, 6, 7, 8, 9, 10
N_GROUPS = 11

VMEM_LIMIT = 56 * 1024 * 1024

GDN_CHUNK = 256
ATT_BLOCK = 512
MOE_TM = 512
MOE_TC = 512


def _cparams(sem):
    return pltpu.CompilerParams(dimension_semantics=sem, vmem_limit_bytes=VMEM_LIMIT)


def _bdot(a, b):
    return jnp.dot(a.astype(BF16), b.astype(BF16), preferred_element_type=F32)


def _bdot_nt(a, b):
    return lax.dot_general(a.astype(BF16), b.astype(BF16), (((1,), (1,)), ((), ())),
                           preferred_element_type=F32)


def _bdot_tn(a, b):
    return lax.dot_general(a.astype(BF16), b.astype(BF16), (((0,), (0,)), ((), ())),
                           preferred_element_type=F32)


def _hdot(a, b):
    return jnp.dot(a, b, preferred_element_type=F32, precision=lax.Precision.HIGHEST)


def _rms(x, w):
    return x * lax.rsqrt(jnp.mean(x * x, axis=-1, keepdims=True) + NORM_EPS) * w


def _rmsnorm_body(x_ref, w_ref, o_ref):
    o_ref[...] = _rms(x_ref[...], w_ref[...]).astype(o_ref.dtype)


def rmsnorm(x, w, out_dtype=BF16, tm=1024):
    t, d = x.shape
    return pl.pallas_call(
        _rmsnorm_body,
        grid=(t // tm,),
        in_specs=[pl.BlockSpec((tm, d), lambda i: (i, 0)), pl.BlockSpec((1, d), lambda i: (0, 0))],
        out_specs=pl.BlockSpec((tm, d), lambda i: (i, 0)),
        out_shape=jax.ShapeDtypeStruct((t, d), out_dtype),
        compiler_params=_cparams(("parallel",)),
        name="rmsnorm",
    )(x, w.reshape(1, d))


def _in_proj_body(h_ref, w_ref, ws_ref, o_ref, os_ref):
    h = h_ref[...]
    o_ref[...] = jnp.dot(h, w_ref[...], preferred_element_type=F32)

    @pl.when(pl.program_id(1) == 0)
    def _():
        os_ref[...] = jnp.dot(h, ws_ref[...], preferred_element_type=F32)


def in_proj(h, w_main, w_small, tm=1024):
    t, d = h.shape
    n = w_main.shape[1]
    return pl.pallas_call(
        _in_proj_body,
        grid=(t // tm, n // GROUP_W),
        in_specs=[pl.BlockSpec((tm, d), lambda i, j: (i, 0)),
                  pl.BlockSpec((d, GROUP_W), lambda i, j: (0, j)),
                  pl.BlockSpec((d, HEAD_W), lambda i, j: (0, 0))],
        out_specs=[pl.BlockSpec((tm, GROUP_W), lambda i, j: (i, j)),
                   pl.BlockSpec((tm, HEAD_W), lambda i, j: (i, 0))],
        out_shape=[jax.ShapeDtypeStruct((t, n), F32), jax.ShapeDtypeStruct((t, HEAD_W), F32)],
        compiler_params=_cparams(("parallel", "arbitrary")),
        name="in_proj",
    )(h, w_main, w_small)


def _group_mean_sq(x, m_ref):
    ss = x * x
    hi = ss.astype(BF16)
    lo = (ss - hi.astype(F32)).astype(BF16)
    m = m_ref[...]
    gs = jnp.dot(hi, m, preferred_element_type=F32) + jnp.dot(lo, m, preferred_element_type=F32)
    return gs * (1.0 / ATT_DIM)


def _attn_prep_body(q_ref, k_ref, v_ref, c_ref, s1_ref, s2_ref, qw_ref, kw_ref, m_ref,
                    qa_ref, qb_ref, ko_ref, vo_ref):
    width = q_ref.shape[1]
    reps = width // HEAD_W
    c = jnp.concatenate([c_ref[...]] * reps, axis=1)
    s1 = jnp.concatenate([s1_ref[...]] * reps, axis=1)
    s2 = jnp.concatenate([s2_ref[...]] * reps, axis=1)

    def norm_rope(x, w):
        y = x * lax.rsqrt(_group_mean_sq(x, m_ref) + NORM_EPS) * w
        half = ROT_DIM // 2
        return y * c + pltpu.roll(y, width - half, 1) * s1 + pltpu.roll(y, half, 1) * s2

    q = norm_rope(q_ref[...], qw_ref[...]) * (ATT_DIM ** -0.5)
    lane = lax.broadcasted_iota(jnp.int32, q.shape, 1)
    first_map = (lane % HEAD_W) < ATT_DIM
    qa_ref[...] = jnp.where(first_map, q, 0.0).astype(BF16)
    qb_ref[...] = jnp.where(first_map, 0.0, q).astype(BF16)
    ko_ref[...] = norm_rope(k_ref[...], kw_ref[...]).astype(BF16)
    vo_ref[...] = v_ref[...].astype(BF16)


def attn_prep(proj, rope_c, rope_s1, rope_s2, q_norm_w, k_norm_w, tm=512):
    t = proj.shape[0]
    w = GROUP_W
    grp = jnp.arange(w) // ATT_DIM
    ones_bd = (grp[:, None] == grp[None, :]).astype(BF16)
    qw = jnp.tile(q_norm_w.astype(F32), w // ATT_DIM).reshape(1, w)
    kw = jnp.tile(k_norm_w.astype(F32), w // ATT_DIM).reshape(1, w)
    col = lambda g: pl.BlockSpec((tm, w), lambda i, g=g: (i, g))
    tab = pl.BlockSpec((tm, HEAD_W), lambda i: (i, 0))
    const = lambda shape: pl.BlockSpec(shape, lambda i: (0, 0))
    out = pl.BlockSpec((tm, w), lambda i: (i, 0))
    return pl.pallas_call(
        _attn_prep_body,
        grid=(t // tm,),
        in_specs=[col(COL_AQ), col(COL_AK), col(COL_AV), tab, tab, tab,
                  const((1, w)), const((1, w)), const((w, w))],
        out_specs=[out, out, out, out],
        out_shape=[jax.ShapeDtypeStruct((t, w), BF16)] * 4,
        compiler_params=_cparams(("parallel",)),
        name="attn_prep",
    )(proj, proj, proj, rope_c, rope_s1, rope_s2, qw, kw, ones_bd)


def _flash_body(qi_ref, ki_ref, qa_ref, qb_ref, k_ref, v_ref, lam_ref, sw_ref, o_ref,
                m_ref, l_ref, acc_ref, *, lam_init):
    p_idx = pl.program_id(2)
    qi = qi_ref[p_idx]
    ki = ki_ref[p_idx]
    tq = qa_ref.shape[0]
    tk = k_ref.shape[0]

    @pl.when(ki == 0)
    def _():
        m_ref[...] = jnp.full(m_ref.shape, -jnp.inf, F32)
        l_ref[...] = jnp.zeros(l_ref.shape, F32)
        acc_ref[...] = jnp.zeros(acc_ref.shape, F32)

    def update(masked):
        k = k_ref[...]
        v = v_ref[...]
        if masked:
            row = lax.broadcasted_iota(jnp.int32, (tq, tk), 0)
            col = lax.broadcasted_iota(jnp.int32, (tq, tk), 1)
            keep = col <= row
        for mp, q_ref in enumerate((qa_ref, qb_ref)):
            s = lax.dot_general(q_ref[...], k, (((1,), (1,)), ((), ())), preferred_element_type=F32)
            if masked:
                s = jnp.where(keep, s, -jnp.inf)
            m_prev = m_ref[mp]
            m_new = jnp.maximum(m_prev, jnp.max(s, axis=-1, keepdims=True))
            alpha = jnp.exp(m_prev - m_new)
            p = jnp.exp(s - m_new)
            l_ref[mp] = alpha * l_ref[mp] + jnp.sum(p, axis=-1, keepdims=True)
            acc_ref[mp] = alpha * acc_ref[mp] + jnp.dot(p.astype(BF16), v, preferred_element_type=F32)
            m_ref[mp] = m_new

    @pl.when(ki < qi)
    def _():
        update(False)

    @pl.when(ki == qi)
    def _():
        update(True)
        lv = lam_ref[...]
        lam = (jnp.exp(jnp.sum(lv[0:1] * lv[1:2], axis=-1, keepdims=True))
               - jnp.exp(jnp.sum(lv[2:3] * lv[3:4], axis=-1, keepdims=True)) + lam_init)
        o = acc_ref[0] / l_ref[0] - lam * (acc_ref[1] / l_ref[1])
        o_ref[...] = (_rms(o, sw_ref[...]) * (1.0 - lam_init)).astype(o_ref.dtype)


def flash_diff_attention(qa, qb, k, v, lam_vec, subln_w, lam_init, batch, seq, blk=ATT_BLOCK):
    t = qa.shape[0]
    nq = seq // blk
    pairs = [(i, j) for i in range(nq) for j in range(i + 1)]
    qi_tab = jnp.asarray([p[0] for p in pairs], jnp.int32)
    ki_tab = jnp.asarray([p[1] for p in pairs], jnp.int32)
    qspec = pl.BlockSpec((blk, HEAD_W), lambda b, h, p, qi, ki: (b * nq + qi[p], h))
    kspec = pl.BlockSpec((blk, HEAD_W), lambda b, h, p, qi, ki: (b * nq + ki[p], h))
    grid_spec = pltpu.PrefetchScalarGridSpec(
        num_scalar_prefetch=2,
        grid=(batch, ATT_HEADS, len(pairs)),
        in_specs=[qspec, qspec, kspec, kspec,
                  pl.BlockSpec((4, ATT_DIM), lambda b, h, p, qi, ki: (0, 0)),
                  pl.BlockSpec((1, HEAD_W), lambda b, h, p, qi, ki: (0, 0))],
        out_specs=qspec,
        scratch_shapes=[pltpu.VMEM((2, blk, 1), F32), pltpu.VMEM((2, blk, 1), F32),
                        pltpu.VMEM((2, blk, HEAD_W), F32)],
    )
    return pl.pallas_call(
        functools.partial(_flash_body, lam_init=lam_init),
        grid_spec=grid_spec,
        out_shape=jax.ShapeDtypeStruct((t, ATT_HEADS * HEAD_W), BF16),
        compiler_params=_cparams(("parallel", "parallel", "arbitrary")),
        name="flash_diff_attention",
    )(qi_tab, ki_tab, qa, qb, k, v, lam_vec.astype(F32), subln_w.astype(F32).reshape(1, HEAD_W))


def _softplus(x):
    return jnp.maximum(x, 0.0) + jnp.log1p(jnp.exp(-jnp.abs(x)))


def _gdn_prep_body(q_ref, k_ref, v_ref, hq_ref, hk_ref, hv_ref, sm_ref, cq_ref, ck_ref, cv_ref,
                   al_ref, dt_ref, qo_ref, ko_ref, vo_ref, so_ref, *, tiles_per_seq):
    tm = q_ref.shape[0]
    seq_start = (pl.program_id(0) % tiles_per_seq) == 0

    def conv_silu(x_ref, halo_ref, w_ref):
        x = x_ref[...]
        halo = jnp.where(seq_start, 0.0, halo_ref[...])
        w = w_ref[...]
        head = x[0:8]
        row8 = lax.broadcasted_iota(jnp.int32, head.shape, 0)
        y = x * w[CONV_K - 1:CONV_K]
        y_head = head * w[CONV_K - 1:CONV_K]
        for j in range(1, CONV_K):
            wj = w[CONV_K - 1 - j:CONV_K - j]
            y = y + pltpu.roll(x, j, 0) * wj
            shifted = jnp.where(row8 < j, pltpu.roll(halo, j, 0), pltpu.roll(head, j, 0))
            y_head = y_head + shifted * wj
        y = jnp.concatenate([y_head, y[8:]], axis=0)
        return y * jax.nn.sigmoid(y)

    def l2n(x):
        parts = []
        for h in range(GDN_HEADS):
            xh = x[:, h * HEAD_W:(h + 1) * HEAD_W]
            parts.append(xh * lax.rsqrt(jnp.sum(xh * xh, axis=-1, keepdims=True) + NORM_EPS))
        return jnp.concatenate(parts, axis=1)

    qo_ref[...] = l2n(conv_silu(q_ref, hq_ref, cq_ref)) * (GDN_DK ** -0.5)
    ko_ref[...] = l2n(conv_silu(k_ref, hk_ref, ck_ref))
    vo_ref[...] = conv_silu(v_ref, hv_ref, cv_ref)
    sm = sm_ref[...]
    lane = lax.broadcasted_iota(jnp.int32, sm.shape, 1)
    beta = jax.nn.sigmoid(sm)
    g = -jnp.exp(al_ref[...]) * _softplus(sm + dt_ref[...])
    so_ref[...] = jnp.where(lane < GDN_HEADS, beta, jnp.where(lane < 2 * GDN_HEADS, g, 0.0))


def gdn_prep(proj, small, conv_w, a_log, dt_bias, seq, tm=512):
    t = proj.shape[0]
    w = GROUP_W
    cw = conv_w.astype(F32)
    pad = lambda vec: jnp.zeros((1, HEAD_W), F32).at[0, GDN_HEADS:2 * GDN_HEADS].set(vec.astype(F32))
    col = lambda g: pl.BlockSpec((tm, w), lambda i, g=g: (i, g))
    halo = lambda g: pl.BlockSpec((8, w), lambda i, g=g: (jnp.maximum(i * (tm // 8) - 1, 0), g))
    const = lambda shape: pl.BlockSpec(shape, lambda i: (0, 0))
    out = pl.BlockSpec((tm, w), lambda i: (i, 0))
    sm_spec = pl.BlockSpec((tm, HEAD_W), lambda i: (i, 0))
    return pl.pallas_call(
        functools.partial(_gdn_prep_body, tiles_per_seq=seq // tm),
        grid=(t // tm,),
        in_specs=[col(COL_BQ), col(COL_BK), col(COL_BV), halo(COL_BQ), halo(COL_BK), halo(COL_BV),
                  sm_spec, const((CONV_K, w)), const((CONV_K, w)), const((CONV_K, w)),
                  const((1, HEAD_W)), const((1, HEAD_W))],
        out_specs=[out, out, out, sm_spec],
        out_shape=[jax.ShapeDtypeStruct((t, w), F32)] * 3 + [jax.ShapeDtypeStruct((t, HEAD_W), F32)],
        compiler_params=_cparams(("parallel",)),
        name="gdn_prep",
    )(proj, proj, proj, proj, proj, proj, small, cw[:, 0:w], cw[:, w:2 * w], cw[:, 2 * w:3 * w],
      pad(a_log), pad(dt_bias))


def _unit_lower_inverse(l_mat, xor_idx, eye):
    c = l_mat.shape[0]
    x = eye
    b = 1
    while b < c:
        e = jnp.where((xor_idx >= b) & (xor_idx < 2 * b), l_mat, 0.0)
        x = x - _bdot(x, _bdot(e, x))
        b *= 2
    return x


def _gdn_body(q_ref, k_ref, v_ref, sm_ref, smt_ref, z_ref, nw_ref, o_ref, state_ref):
    c = q_ref.shape[0]

    @pl.when(pl.program_id(1) == 0)
    def _():
        state_ref[...] = jnp.zeros(state_ref.shape, F32)

    row = lax.broadcasted_iota(jnp.int32, (c, c), 0)
    col = lax.broadcasted_iota(jnp.int32, (c, c), 1)
    incl = row >= col
    strict = row > col
    xor_idx = row ^ col
    eye = jnp.where(row == col, 1.0, 0.0).astype(F32)
    sm = sm_ref[...]
    gc_col = _hdot(jnp.where(incl, 1.0, 0.0).astype(F32), sm)
    gc_row = _hdot(smt_ref[...], jnp.where(row <= col, 1.0, 0.0).astype(F32))
    nw = nw_ref[...]
    for h in range(GDN_HEADS):
        sl = slice(h * HEAD_W, (h + 1) * HEAD_W)
        q = q_ref[:, sl]
        k = k_ref[:, sl]
        v = v_ref[:, sl]
        beta = sm[:, h:h + 1]
        gcol = gc_col[:, GDN_HEADS + h:GDN_HEADS + h + 1]
        grow = gc_row[GDN_HEADS + h:GDN_HEADS + h + 1, :]
        decay = jnp.where(incl, jnp.exp(jnp.minimum(gcol - grow, 0.0)), 0.0)
        kb = k * beta
        l_mat = jnp.where(strict, _bdot_nt(kb, k) * decay, 0.0)
        a_qk = _bdot_nt(q, k) * decay
        eg = jnp.exp(gcol)
        rhs = jnp.concatenate([v * beta, kb * eg], axis=1)
        sol = _bdot(_unit_lower_inverse(l_mat, xor_idx, eye), rhs)
        u = sol[:, :HEAD_W]
        w = sol[:, HEAD_W:]
        state = state_ref[h]
        v_new = u - _bdot(w, state)
        o = _bdot(q * eg, state) + _bdot(a_qk, v_new)
        g_last = gcol[c - 1:c, :]
        k_dec = k * jnp.exp(g_last - gcol)
        state_ref[h] = state * jnp.exp(g_last) + _bdot_tn(k_dec, v_new)
        z = z_ref[:, sl]
        o_ref[:, sl] = (_rms(o, nw) * (z * jax.nn.sigmoid(z))).astype(o_ref.dtype)


def gdn_recurrence(gq, gk, gv, gsm, proj, gdn_norm_w, batch, seq, chunk=GDN_CHUNK):
    t = gq.shape[0]
    nc = seq // chunk
    w = GROUP_W
    gsm_t = gsm[:, :8].T
    blk = pl.BlockSpec((chunk, w), lambda b, c: (b * nc + c, 0))
    return pl.pallas_call(
        _gdn_body,
        grid=(batch, nc),
        in_specs=[blk, blk, blk,
                  pl.BlockSpec((chunk, HEAD_W), lambda b, c: (b * nc + c, 0)),
                  pl.BlockSpec((8, chunk), lambda b, c: (0, b * nc + c)),
                  pl.BlockSpec((chunk, w), lambda b, c: (b * nc + c, COL_BZ)),
                  pl.BlockSpec((1, HEAD_W), lambda b, c: (0, 0))],
        out_specs=blk,
        out_shape=jax.ShapeDtypeStruct((t, w), BF16),
        scratch_shapes=[pltpu.VMEM((GDN_HEADS, GDN_DK, HEAD_W), F32)],
        compiler_params=_cparams(("parallel", "arbitrary")),
        name="gdn_recurrence",
    )(gq, gk, gv, gsm, gsm_t, proj, gdn_norm_w.astype(F32).reshape(1, HEAD_W))


def _top2_route(logits):
    lane = lax.broadcasted_iota(jnp.int32, logits.shape, 1).astype(F32)
    big = float(logits.shape[1])
    m1 = jnp.max(logits, axis=-1, keepdims=True)
    i1 = jnp.min(jnp.where(logits == m1, lane, big), axis=-1, keepdims=True)
    rest = jnp.where(lane == i1, -jnp.inf, logits)
    m2 = jnp.max(rest, axis=-1, keepdims=True)
    i2 = jnp.min(jnp.where(rest == m2, lane, big), axis=-1, keepdims=True)
    e = jnp.exp(m2 - m1)
    g1 = 1.0 / (1.0 + e)
    g2 = e / (1.0 + e)
    return jnp.where(lane == 0, i1, jnp.where(lane == 1, i2, jnp.where(lane == 2, g1, jnp.where(lane == 3, g2, 0.0))))


def _mixer_out_body(*refs, with_router):
    if with_router:
        (oa_ref, ob_ref, ga_ref, gb_ref, x_ref, wa_ref, wb_ref, wo_ref, nw_ref, rw_ref, rb_ref,
         xo_ref, ho_ref, ro_ref) = refs
    else:
        (oa_ref, ob_ref, ga_ref, gb_ref, x_ref, wa_ref, wb_ref, wo_ref, nw_ref, xo_ref, ho_ref) = refs
    ya = jnp.dot(oa_ref[...], wa_ref[...], preferred_element_type=F32)
    yb = jnp.dot(ob_ref[...], wb_ref[...], preferred_element_type=F32)
    merged = jax.nn.sigmoid(ga_ref[...]) * ya + jax.nn.sigmoid(gb_ref[...]) * yb
    x_new = x_ref[...] + jnp.dot(merged.astype(BF16), wo_ref[...], preferred_element_type=F32)
    xo_ref[...] = x_new
    hn = _rms(x_new, nw_ref[...])
    ho_ref[...] = hn.astype(ho_ref.dtype)
    if with_router:
        ro_ref[...] = _top2_route(_hdot(hn, rw_ref[...]) + rb_ref[...])


def mixer_out(oa, ob, proj, x, w_a, w_b, w_o, norm_w, router=None, tm=512):
    t, d = x.shape
    w = GROUP_W
    row = lambda width, g=0: pl.BlockSpec((tm, width), lambda i, g=g: (i, g))
    const = lambda shape: pl.BlockSpec(shape, lambda i: (0, 0))
    in_specs = [row(w), row(w), row(d, COL_GA * w // d), row(d, COL_GB * w // d), row(d),
                const((w, d)), const((w, d)), const((d, d)), const((1, d))]
    args = [oa, ob, proj, proj, x, w_a, w_b, w_o, norm_w.astype(F32).reshape(1, d)]
    out_specs = [row(d), row(d)]
    out_shape = [jax.ShapeDtypeStruct((t, d), F32),
                 jax.ShapeDtypeStruct((t, d), F32 if router is not None else BF16)]
    if router is not None:
        rw, rb = router
        rw_pad = jnp.zeros((d, HEAD_W), F32).at[:, :N_EXPERTS].set(rw.astype(F32))
        rb_pad = jnp.full((1, HEAD_W), -jnp.inf, F32).at[0, :N_EXPERTS].set(rb.astype(F32))
        in_specs += [const((d, HEAD_W)), const((1, HEAD_W))]
        args += [rw_pad, rb_pad]
        out_specs.append(row(HEAD_W))
        out_shape.append(jax.ShapeDtypeStruct((t, HEAD_W), F32))
    return pl.pallas_call(
        functools.partial(_mixer_out_body, with_router=router is not None),
        grid=(t // tm,),
        in_specs=in_specs,
        out_specs=out_specs,
        out_shape=out_shape,
        compiler_params=_cparams(("parallel",)),
        name="mixer_out",
    )(*args)


def _ffn_body(*refs, with_norm, tc):
    if with_norm:
        h_ref, x_ref, wg_ref, wu_ref, wd_ref, nw_ref, xo_ref, ho_ref, acc_ref = refs
    else:
        h_ref, x_ref, wg_ref, wu_ref, wd_ref, xo_ref, acc_ref = refs
    h = h_ref[...]
    d_ff = wg_ref.shape[1]
    for ci, c0 in enumerate(range(0, d_ff, tc)):
        g = jnp.dot(h, wg_ref[:, c0:c0 + tc], preferred_element_type=F32)
        u = jnp.dot(h, wu_ref[:, c0:c0 + tc], preferred_element_type=F32)
        a = (g * jax.nn.sigmoid(g) * u).astype(BF16)
        part = jnp.dot(a, wd_ref[c0:c0 + tc, :], preferred_element_type=F32)
        if ci == 0:
            acc_ref[...] = x_ref[...] + part
        else:
            acc_ref[...] += part
    x_new = acc_ref[...]
    xo_ref[...] = x_new
    if with_norm:
        ho_ref[...] = _rms(x_new, nw_ref[...]).astype(ho_ref.dtype)


def dense_ffn(h, x, w_g, w_u, w_d, next_norm_w=None, tm=512, tc=256):
    t, d = x.shape
    d_ff = w_g.shape[1]
    row = pl.BlockSpec((tm, d), lambda i: (i, 0))
    const = lambda shape: pl.BlockSpec(shape, lambda i: (0, 0), pipeline_mode=pl.Buffered(1))
    in_specs = [row, row, const((d, d_ff)), const((d, d_ff)), const((d_ff, d))]
    args = [h, x, w_g, w_u, w_d]
    out_specs = [row]
    out_shape = [jax.ShapeDtypeStruct((t, d), F32)]
    if next_norm_w is not None:
        in_specs.append(pl.BlockSpec((1, d), lambda i: (0, 0)))
        args.append(next_norm_w.astype(F32).reshape(1, d))
        out_specs.append(row)
        out_shape.append(jax.ShapeDtypeStruct((t, d), BF16))
    return pl.pallas_call(
        functools.partial(_ffn_body, with_norm=next_norm_w is not None, tc=tc),
        grid=(t // tm,),
        in_specs=in_specs,
        out_specs=out_specs,
        out_shape=out_shape,
        scratch_shapes=[pltpu.VMEM((tm, d), F32)],
        compiler_params=_cparams(("parallel",)),
        name="dense_ffn",
    )(*args)


def _dispatch_body(dest_ref, h_ref, zero_ref, xb_ref, sem, *, tb):
    del zero_ref
    base = pl.program_id(0) * tb

    def copy(r, k):
        t = base + r
        return pltpu.make_async_copy(h_ref.at[pl.ds(t, 1)], xb_ref.at[pl.ds(dest_ref[2 * t + k], 1)], sem)

    def start(r, carry):
        copy(r, 0).start()
        copy(r, 1).start()
        return carry

    def wait(r, carry):
        copy(r, 0).wait()
        copy(r, 1).wait()
        return carry

    lax.fori_loop(0, tb, start, 0)
    lax.fori_loop(0, tb, wait, 0)


def moe_dispatch(h, dest, n_slots, tb=512):
    t, d = h.shape
    grid_spec = pltpu.PrefetchScalarGridSpec(
        num_scalar_prefetch=1,
        grid=(t // tb,),
        in_specs=[pl.BlockSpec(memory_space=pl.ANY), pl.BlockSpec(memory_space=pl.ANY)],
        out_specs=pl.BlockSpec(memory_space=pl.ANY),
        scratch_shapes=[pltpu.SemaphoreType.DMA(())],
    )
    return pl.pallas_call(
        functools.partial(_dispatch_body, tb=tb),
        grid_spec=grid_spec,
        out_shape=jax.ShapeDtypeStruct((n_slots, d), h.dtype),
        input_output_aliases={2: 0},
        compiler_params=pltpu.CompilerParams(dimension_semantics=("arbitrary",), has_side_effects=True),
        name="moe_dispatch",
    )(dest, h, jnp.zeros((n_slots, d), h.dtype))


def _expert_body(be_ref, nb_ref, x_ref, wg_ref, wu_ref, wd_ref, o_ref, xb_ref, acc_ref):
    blk = pl.program_id(0)
    c = pl.program_id(1)
    live = blk < nb_ref[0]

    @pl.when(jnp.logical_and(live, c == 0))
    def _():
        xb_ref[...] = x_ref[...].astype(BF16)

    @pl.when(live)
    def _():
        xb = xb_ref[...]
        g = jnp.dot(xb, wg_ref[0], preferred_element_type=F32)
        u = jnp.dot(xb, wu_ref[0], preferred_element_type=F32)
        a = (g * jax.nn.sigmoid(g) * u).astype(BF16)
        part = jnp.dot(a, wd_ref[0], preferred_element_type=F32)

        @pl.when(c == 0)
        def _():
            acc_ref[...] = part

        @pl.when(c > 0)
        def _():
            acc_ref[...] += part

    last = c == pl.num_programs(1) - 1

    @pl.when(jnp.logical_and(live, last))
    def _():
        o_ref[...] = acc_ref[...]

    @pl.when(jnp.logical_and(jnp.logical_not(live), last))
    def _():
        o_ref[...] = jnp.zeros(o_ref.shape, F32)


def moe_experts(xb, block_e, n_live, w_g, w_u, w_d, tm=MOE_TM, tc=MOE_TC):
    n_slots, d = xb.shape
    d_e = w_g.shape[2]
    grid_spec = pltpu.PrefetchScalarGridSpec(
        num_scalar_prefetch=2,
        grid=(n_slots // tm, d_e // tc),
        in_specs=[pl.BlockSpec((tm, d), lambda b, c, be, nb: (b, 0)),
                  pl.BlockSpec((1, d, tc), lambda b, c, be, nb: (be[b], 0, c)),
                  pl.BlockSpec((1, d, tc), lambda b, c, be, nb: (be[b], 0, c)),
                  pl.BlockSpec((1, tc, d), lambda b, c, be, nb: (be[b], c, 0))],
        out_specs=pl.BlockSpec((tm, d), lambda b, c, be, nb: (b, 0)),
        scratch_shapes=[pltpu.VMEM((tm, d), BF16), pltpu.VMEM((tm, d), F32)],
    )
    return pl.pallas_call(
        _expert_body,
        grid_spec=grid_spec,
        out_shape=jax.ShapeDtypeStruct((n_slots, d), F32),
        compiler_params=_cparams(("arbitrary", "arbitrary")),
        name="moe_experts",
    )(block_e, n_live, xb, w_g, w_u, w_d)


def _combine_body(dest_ref, x_ref, r_ref, yb_ref, o_ref, buf_ref, sem, *, tb):
    base = pl.program_id(0) * tb

    def copy(r, k):
        src = yb_ref.at[pl.ds(dest_ref[2 * (base + r) + k], 1)]
        return pltpu.make_async_copy(src, buf_ref.at[k, pl.ds(r, 1)], sem)

    def start(r, carry):
        copy(r, 0).start()
        copy(r, 1).start()
        return carry

    def wait(r, carry):
        copy(r, 0).wait()
        copy(r, 1).wait()
        return carry

    lax.fori_loop(0, tb, start, 0)
    lax.fori_loop(0, tb, wait, 0)
    route = r_ref[...]
    o_ref[...] = x_ref[...] + route[:, 2:3] * buf_ref[0] + route[:, 3:4] * buf_ref[1]


def moe_combine(x, route, yb, dest, tb=256):
    t, d = x.shape
    grid_spec = pltpu.PrefetchScalarGridSpec(
        num_scalar_prefetch=1,
        grid=(t // tb,),
        in_specs=[pl.BlockSpec((tb, d), lambda i, dest: (i, 0)),
                  pl.BlockSpec((tb, HEAD_W), lambda i, dest: (i, 0)),
                  pl.BlockSpec(memory_space=pl.ANY)],
        out_specs=pl.BlockSpec((tb, d), lambda i, dest: (i, 0)),
        scratch_shapes=[pltpu.VMEM((2, tb, d), F32), pltpu.SemaphoreType.DMA(())],
    )
    return pl.pallas_call(
        functools.partial(_combine_body, tb=tb),
        grid_spec=grid_spec,
        out_shape=jax.ShapeDtypeStruct((t, d), F32),
        compiler_params=_cparams(("arbitrary",)),
        name="moe_combine",
    )(dest, x, route, yb)


def moe_layer(hn, x, route, w_g, w_u, w_d, tm=MOE_TM):
    t, d = x.shape
    n_assign = 2 * t
    e_flat = route[:, :2].astype(jnp.int32).reshape(n_assign)
    onehot = (e_flat[:, None] == jnp.arange(N_EXPERTS, dtype=jnp.int32)[None, :]).astype(jnp.int32)
    incl = jnp.cumsum(onehot, axis=0)
    rank = jnp.sum((incl - onehot) * onehot, axis=1)
    counts = incl[-1]
    padded = (counts + tm - 1) // tm * tm
    pad_end = jnp.cumsum(padded)
    pad_start = pad_end - padded
    dest = (pad_start[e_flat] + rank).astype(jnp.int32)
    n_blocks = -(-n_assign // tm) + N_EXPERTS
    block_e = jnp.minimum(jnp.searchsorted(pad_end, jnp.arange(n_blocks, dtype=jnp.int32) * tm, side='right'),
                          N_EXPERTS - 1).astype(jnp.int32)
    n_live = (pad_end[-1:] // tm).astype(jnp.int32)
    xb = moe_dispatch(hn, dest, n_blocks * tm)
    yb = moe_experts(xb, block_e, n_live, w_g, w_u, w_d)
    return moe_combine(x, route, yb, dest)


def _rope_coefficients(positions):
    half = ROT_DIM // 2
    inv_freq = ROPE_THETA ** (-jnp.arange(0, ROT_DIM, 2, dtype=F32) / ROT_DIM)
    ang = positions.astype(F32).reshape(-1, 1) * inv_freq
    cos, sin = jnp.cos(ang), jnp.sin(ang)
    t = ang.shape[0]
    ones = jnp.ones((t, ATT_DIM - ROT_DIM), F32)
    zeros_h = jnp.zeros((t, half), F32)
    zeros_r = jnp.zeros((t, ATT_DIM - ROT_DIM), F32)
    c = jnp.concatenate([cos, cos, ones], axis=1)
    s1 = jnp.concatenate([-sin, zeros_h, zeros_r], axis=1)
    s2 = jnp.concatenate([zeros_h, sin, zeros_r], axis=1)
    return tuple(jnp.concatenate([a, a], axis=1) for a in (c, s1, s2))


def _split_w_in(w_in):
    att = 3 * GROUP_W
    conv = 3 * GROUP_W
    o_z = att + conv
    o_small = o_z + GROUP_W
    o_ga = o_small + 2 * GDN_HEADS
    d = w_in.shape[0]
    main = jnp.concatenate([w_in[:, o_ga:], w_in[:, :o_small]], axis=1).astype(BF16)
    small = jnp.zeros((d, HEAD_W), F32).at[:, :2 * GDN_HEADS].set(w_in[:, o_small:o_ga]).astype(BF16)
    return main, small


def kernel(x, positions, norm_mix_w, w_in, q_norm_w, k_norm_w, lam_vec, subln_w, conv_w, a_log, dt_bias,
           gdn_norm_w, w_branch_a, w_branch_b, w_out, norm_ffn_w, ffn_w_gate_up, ffn_w_down, router_w,
           router_b, moe_w_gate_up, moe_w_down):
    batch, seq, d = x.shape
    depth = w_in.shape[0]
    t = batch * seq
    xt = x.reshape(t, d).astype(F32)
    rope_c, rope_s1, rope_s2 = _rope_coefficients(positions)
    h = rmsnorm(xt, norm_mix_w[0].astype(F32))
    for layer in range(depth):
        lam_init = 0.8 - 0.6 * math.exp(-0.3 * layer)
        is_moe = layer % 2 == 1
        j = layer // 2
        w_main, w_small = _split_w_in(w_in[layer])
        proj, small = in_proj(h, w_main, w_small)
        qa, qb, kk, vv = attn_prep(proj, rope_c, rope_s1, rope_s2, q_norm_w[layer], k_norm_w[layer])
        oa = flash_diff_attention(qa, qb, kk, vv, lam_vec[layer], subln_w[layer], lam_init, batch, seq)
        gq, gk, gv, gsm = gdn_prep(proj, small, conv_w[layer], a_log[layer], dt_bias[layer], seq)
        ob = gdn_recurrence(gq, gk, gv, gsm, proj, gdn_norm_w[layer], batch, seq)
        router = (router_w[j], router_b[j]) if is_moe else None
        outs = mixer_out(oa, ob, proj, xt, w_branch_a[layer].astype(BF16), w_branch_b[layer].astype(BF16),
                         w_out[layer].astype(BF16), norm_ffn_w[layer], router)
        if is_moe:
            xt, hn, route = outs
            d_e = moe_w_down.shape[2]
            wgu = moe_w_gate_up[j].astype(BF16)
            xt = moe_layer(hn, xt, route, wgu[:, :, :d_e], wgu[:, :, d_e:], moe_w_down[j].astype(BF16))
            if layer + 1 < depth:
                h = rmsnorm(xt, norm_mix_w[layer + 1].astype(F32))
        else:
            xt, hn = outs
            d_ff = ffn_w_down.shape[1]
            wgu = ffn_w_gate_up[j].astype(BF16)
            nxt = norm_mix_w[layer + 1] if layer + 1 < depth else None
            res = dense_ffn(hn, xt, wgu[:, :d_ff], wgu[:, d_ff:], ffn_w_down[j].astype(BF16), nxt)
            if nxt is not None:
                xt, h = res
            else:
                xt = res[0]
    return xt.reshape(batch, seq, d)
```

```python
import functools
import math

import jax
import jax.numpy as jnp
from jax import lax
from jax.experimental import pallas as pl
from jax.experimental.pallas import tpu as pltpu

F32 = jnp.float32
BF16 = jnp.bfloat16

ATT_HEADS = 4
ATT_DIM = 64
ROT_DIM = ATT_DIM // 4
ROPE_THETA = 500000.0
GDN_HEADS = 4
GDN_DK = 128
CONV_K = 4
N_EXPERTS = 8
NORM_EPS = 1e-6

HEAD_W = 128
GROUP_W = 512
COL_GA, COL_GB, COL_AQ, COL_AK, COL_AV, COL_BQ, COL_BK, COL_BV, COL_BZ = 0, 2, 4, 5, 6, 7, 8, 9, 10
N_GROUPS = 11

VMEM_LIMIT = 56 * 1024 * 1024

GDN_CHUNK = 256
ATT_BLOCK = 1024
ATT_CHUNK = 256
LOG2E = math.log2(math.e)
BOUND_SLACK = 1.01
MAX_SAFE_LOG2_BOUND = 56.0
MOE_TM = 448
MOE_TC = 512


def _cparams(sem):
    return pltpu.CompilerParams(dimension_semantics=sem, vmem_limit_bytes=VMEM_LIMIT)


def _bdot(a, b):
    return jnp.dot(a.astype(BF16), b.astype(BF16), preferred_element_type=F32)


def _bdot_nt(a, b):
    return lax.dot_general(a.astype(BF16), b.astype(BF16), (((1,), (1,)), ((), ())),
                           preferred_element_type=F32)


def _bdot_tn(a, b):
    return lax.dot_general(a.astype(BF16), b.astype(BF16), (((0,), (0,)), ((), ())),
                           preferred_element_type=F32)


def _hdot(a, b):
    return jnp.dot(a, b, preferred_element_type=F32, precision=lax.Precision.HIGHEST)


def _rms(x, w):
    return x * lax.rsqrt(jnp.mean(x * x, axis=-1, keepdims=True) + NORM_EPS) * w


def _rmsnorm_body(x_ref, w_ref, o_ref):
    o_ref[...] = _rms(x_ref[...], w_ref[...]).astype(o_ref.dtype)


def rmsnorm(x, w, out_dtype=BF16, tm=1024):
    t, d = x.shape
    return pl.pallas_call(
        _rmsnorm_body,
        grid=(t // tm,),
        in_specs=[pl.BlockSpec((tm, d), lambda i: (i, 0)), pl.BlockSpec((1, d), lambda i: (0, 0))],
        out_specs=pl.BlockSpec((tm, d), lambda i: (i, 0)),
        out_shape=jax.ShapeDtypeStruct((t, d), out_dtype),
        compiler_params=_cparams(("parallel",)),
        name="rmsnorm",
    )(x, w.reshape(1, d))


def _in_proj_body(h_ref, w_ref, ws_ref, o_ref, os_ref):
    h = h_ref[...]
    o_ref[...] = jnp.dot(h, w_ref[...], preferred_element_type=F32)

    @pl.when(pl.program_id(1) == 0)
    def _():
        os_ref[...] = jnp.dot(h, ws_ref[...], preferred_element_type=F32)


def in_proj(h, w_main, w_small, tm=1024):
    t, d = h.shape
    n = w_main.shape[1]
    return pl.pallas_call(
        _in_proj_body,
        grid=(t // tm, n // GROUP_W),
        in_specs=[pl.BlockSpec((tm, d), lambda i, j: (i, 0)),
                  pl.BlockSpec((d, GROUP_W), lambda i, j: (0, j)),
                  pl.BlockSpec((d, HEAD_W), lambda i, j: (0, 0))],
        out_specs=[pl.BlockSpec((tm, GROUP_W), lambda i, j: (i, j)),
                   pl.BlockSpec((tm, HEAD_W), lambda i, j: (i, 0))],
        out_shape=[jax.ShapeDtypeStruct((t, n), F32), jax.ShapeDtypeStruct((t, HEAD_W), F32)],
        compiler_params=_cparams(("parallel", "arbitrary")),
        name="in_proj",
    )(h, w_main, w_small)


def _group_mean_sq(x, m_ref):
    ss = x * x
    hi = ss.astype(BF16)
    lo = (ss - hi.astype(F32)).astype(BF16)
    m = m_ref[...]
    gs = jnp.dot(hi, m, preferred_element_type=F32) + jnp.dot(lo, m, preferred_element_type=F32)
    return gs * (1.0 / ATT_DIM)


def _attn_prep_body(q_ref, k_ref, v_ref, c_ref, s1_ref, s2_ref, qw_ref, kw_ref, m_ref,
                    qs_ref, ko_ref, vx_ref):
    width = q_ref.shape[1]
    reps = width // HEAD_W
    c = jnp.concatenate([c_ref[...]] * reps, axis=1)
    s1 = jnp.concatenate([s1_ref[...]] * reps, axis=1)
    s2 = jnp.concatenate([s2_ref[...]] * reps, axis=1)

    def norm_rope(x, w):
        y = x * lax.rsqrt(_group_mean_sq(x, m_ref) + NORM_EPS) * w
        half = ROT_DIM // 2
        return y * c + pltpu.roll(y, width - half, 1) * s1 + pltpu.roll(y, half, 1) * s2

    q = norm_rope(q_ref[...], qw_ref[...]) * (ATT_DIM ** -0.5 * LOG2E)
    lane = lax.broadcasted_iota(jnp.int32, q.shape, 1)
    first_map = (lane % HEAD_W) < ATT_DIM
    qs_ref[0] = jnp.where(first_map, q, 0.0).astype(BF16)
    qs_ref[1] = jnp.where(first_map, 0.0, q).astype(BF16)
    ko_ref[...] = norm_rope(k_ref[...], kw_ref[...]).astype(BF16)
    v = v_ref[...].astype(BF16)
    ones = jnp.ones((v.shape[0], HEAD_W), BF16)
    parts = []
    for h in range(reps):
        parts += [v[:, h * HEAD_W:(h + 1) * HEAD_W], ones]
    vx_ref[...] = jnp.concatenate(parts, axis=1)


def attn_prep(proj, rope_c, rope_s1, rope_s2, q_norm_w, k_norm_w, tm=512):
    t = proj.shape[0]
    w = GROUP_W
    grp = jnp.arange(w) // ATT_DIM
    ones_bd = (grp[:, None] == grp[None, :]).astype(BF16)
    qw = jnp.tile(q_norm_w.astype(F32), w // ATT_DIM).reshape(1, w)
    kw = jnp.tile(k_norm_w.astype(F32), w // ATT_DIM).reshape(1, w)
    col = lambda g: pl.BlockSpec((tm, w), lambda i, g=g: (i, g))
    tab = pl.BlockSpec((tm, HEAD_W), lambda i: (i, 0))
    const = lambda shape: pl.BlockSpec(shape, lambda i: (0, 0))
    return pl.pallas_call(
        _attn_prep_body,
        grid=(t // tm,),
        in_specs=[col(COL_AQ), col(COL_AK), col(COL_AV), tab, tab, tab,
                  const((1, w)), const((1, w)), const((w, w))],
        out_specs=[pl.BlockSpec((2, tm, w), lambda i: (0, i, 0)), pl.BlockSpec((tm, w), lambda i: (i, 0)),
                   pl.BlockSpec((tm, 2 * w), lambda i: (i, 0))],
        out_shape=[jax.ShapeDtypeStruct((2, t, w), BF16), jax.ShapeDtypeStruct((t, w), BF16),
                   jax.ShapeDtypeStruct((t, 2 * w), BF16)],
        compiler_params=_cparams(("parallel",)),
        name="attn_prep",
    )(proj, proj, proj, rope_c, rope_s1, rope_s2, qw, kw, ones_bd)


def _flash_body(qi_ref, ki_ref, fast_ref, bound_ref, q_ref, k_ref, vx_ref, lam_ref, sw_ref, o_ref,
                m_ref, acc_ref, *, lam_init, chunk):
    p_idx = pl.program_id(2)
    qi = qi_ref[p_idx]
    ki = ki_ref[p_idx]
    tq = q_ref.shape[1]
    tk = k_ref.shape[0]
    fast = fast_ref[0] == 1
    diag = ki == qi

    @pl.when(ki == 0)
    def _():
        m_ref[...] = jnp.full(m_ref.shape, -jnp.inf, F32)
        acc_ref[...] = jnp.zeros(acc_ref.shape, F32)

    def keep_mask(c0, width):
        row = lax.broadcasted_iota(jnp.int32, (2 * tq, width), 0)
        col = lax.broadcasted_iota(jnp.int32, (2 * tq, width), 1) + c0
        return col <= jnp.where(row >= tq, row - tq, row)

    def update_bounded(masked):
        q2 = q_ref[...].reshape(2 * tq, HEAD_W)
        bound = bound_ref[0]
        total = None
        for c0 in range(0, tk, chunk):
            s = lax.dot_general(q2, k_ref[c0:c0 + chunk, :], (((1,), (1,)), ((), ())),
                                preferred_element_type=F32)
            p = jnp.exp2(s - bound)
            if masked:
                p = jnp.where(keep_mask(c0, chunk), p, 0.0)
            part = jnp.dot(p.astype(BF16), vx_ref[c0:c0 + chunk, :], preferred_element_type=F32)
            total = part if total is None else total + part
        acc_ref[...] += total

    def update_running_max(masked):
        q2 = q_ref[...].reshape(2 * tq, HEAD_W)
        s = lax.dot_general(q2, k_ref[...], (((1,), (1,)), ((), ())), preferred_element_type=F32)
        if masked:
            s = jnp.where(keep_mask(0, tk), s, -jnp.inf)
        m_prev = m_ref[...]
        m_new = jnp.maximum(m_prev, jnp.max(s, axis=-1, keepdims=True))
        p = jnp.exp2(s - m_new)
        acc_ref[...] = (jnp.exp2(m_prev - m_new) * acc_ref[...]
                        + jnp.dot(p.astype(BF16), vx_ref[...], preferred_element_type=F32))
        m_ref[...] = m_new

    for use_fast, update in ((True, update_bounded), (False, update_running_max)):
        path = fast if use_fast else jnp.logical_not(fast)

        @pl.when(jnp.logical_and(path, jnp.logical_not(diag)))
        def _(update=update):
            update(False)

        @pl.when(jnp.logical_and(path, diag))
        def _(update=update):
            update(True)

    @pl.when(diag)
    def _():
        lv = lam_ref[...]
        lam = (jnp.exp(jnp.sum(lv[0:1] * lv[1:2], axis=-1, keepdims=True))
               - jnp.exp(jnp.sum(lv[2:3] * lv[3:4], axis=-1, keepdims=True)) + lam_init)
        o = (acc_ref[0:tq, 0:HEAD_W] / acc_ref[0:tq, HEAD_W:]
             - lam * (acc_ref[tq:, 0:HEAD_W] / acc_ref[tq:, HEAD_W:]))
        o_ref[...] = (_rms(o, sw_ref[...]) * (1.0 - lam_init)).astype(o_ref.dtype)


def flash_diff_attention(qs, k, vx, q_norm_w, k_norm_w, lam_vec, subln_w, lam_init, batch, seq,
                         blk=ATT_BLOCK, chunk=ATT_CHUNK):
    t = k.shape[0]
    nq = seq // blk
    pairs = [(i, j) for i in range(nq) for j in range(i + 1)]
    qi_tab = jnp.asarray([p[0] for p in pairs], jnp.int32)
    ki_tab = jnp.asarray([p[1] for p in pairs], jnp.int32)
    bound = (ATT_DIM ** 0.5) * jnp.max(jnp.abs(q_norm_w.astype(F32))) * jnp.max(jnp.abs(k_norm_w.astype(F32)))
    bound = bound * (LOG2E * BOUND_SLACK)
    fast = (bound <= MAX_SAFE_LOG2_BOUND).astype(jnp.int32).reshape(1)
    idx = lambda f: (lambda b, h, p, qi, ki, fast: f(b, h, qi[p], ki[p]))
    grid_spec = pltpu.PrefetchScalarGridSpec(
        num_scalar_prefetch=3,
        grid=(batch, ATT_HEADS, len(pairs)),
        in_specs=[pl.BlockSpec(memory_space=pltpu.SMEM),
                  pl.BlockSpec((2, blk, HEAD_W), idx(lambda b, h, qi, ki: (0, b * nq + qi, h))),
                  pl.BlockSpec((blk, HEAD_W), idx(lambda b, h, qi, ki: (b * nq + ki, h))),
                  pl.BlockSpec((blk, 2 * HEAD_W), idx(lambda b, h, qi, ki: (b * nq + ki, h))),
                  pl.BlockSpec((4, ATT_DIM), idx(lambda b, h, qi, ki: (0, 0))),
                  pl.BlockSpec((1, HEAD_W), idx(lambda b, h, qi, ki: (0, 0)))],
        out_specs=pl.BlockSpec((blk, HEAD_W), idx(lambda b, h, qi, ki: (b * nq + qi, h))),
        scratch_shapes=[pltpu.VMEM((2 * blk, 1), F32), pltpu.VMEM((2 * blk, 2 * HEAD_W), F32)],
    )
    return pl.pallas_call(
        functools.partial(_flash_body, lam_init=lam_init, chunk=chunk),
        grid_spec=grid_spec,
        out_shape=jax.ShapeDtypeStruct((t, ATT_HEADS * HEAD_W), BF16),
        compiler_params=_cparams(("parallel", "parallel", "arbitrary")),
        name="flash_diff_attention",
    )(qi_tab, ki_tab, fast, bound.reshape(1), qs, k, vx, lam_vec.astype(F32),
      subln_w.astype(F32).reshape(1, HEAD_W))


def _softplus(x):
    return jnp.maximum(x, 0.0) + jnp.log1p(jnp.exp(-jnp.abs(x)))


def _gdn_prep_body(q_ref, k_ref, v_ref, hq_ref, hk_ref, hv_ref, sm_ref, cq_ref, ck_ref, cv_ref,
                   al_ref, dt_ref, qo_ref, ko_ref, vo_ref, so_ref, *, tiles_per_seq):
    tm = q_ref.shape[0]
    seq_start = (pl.program_id(0) % tiles_per_seq) == 0

    def conv_silu(x_ref, halo_ref, w_ref):
        x = x_ref[...]
        halo = jnp.where(seq_start, 0.0, halo_ref[...])
        w = w_ref[...]
        head = x[0:8]
        row8 = lax.broadcasted_iota(jnp.int32, head.shape, 0)
        y = x * w[CONV_K - 1:CONV_K]
        y_head = head * w[CONV_K - 1:CONV_K]
        for j in range(1, CONV_K):
            wj = w[CONV_K - 1 - j:CONV_K - j]
            y = y + pltpu.roll(x, j, 0) * wj
            shifted = jnp.where(row8 < j, pltpu.roll(halo, j, 0), pltpu.roll(head, j, 0))
            y_head = y_head + shifted * wj
        y = jnp.concatenate([y_head, y[8:]], axis=0)
        return y * jax.nn.sigmoid(y)

    def l2n(x):
        parts = []
        for h in range(GDN_HEADS):
            xh = x[:, h * HEAD_W:(h + 1) * HEAD_W]
            parts.append(xh * lax.rsqrt(jnp.sum(xh * xh, axis=-1, keepdims=True) + NORM_EPS))
        return jnp.concatenate(parts, axis=1)

    qo_ref[...] = l2n(conv_silu(q_ref, hq_ref, cq_ref)) * (GDN_DK ** -0.5)
    ko_ref[...] = l2n(conv_silu(k_ref, hk_ref, ck_ref))
    vo_ref[...] = conv_silu(v_ref, hv_ref, cv_ref)
    sm = sm_ref[...]
    lane = lax.broadcasted_iota(jnp.int32, sm.shape, 1)
    beta = jax.nn.sigmoid(sm)
    g = -jnp.exp(al_ref[...]) * _softplus(sm + dt_ref[...])
    so_ref[...] = jnp.where(lane < GDN_HEADS, beta, jnp.where(lane < 2 * GDN_HEADS, g, 0.0))


def gdn_prep(proj, small, conv_w, a_log, dt_bias, seq, tm=512):
    t = proj.shape[0]
    w = GROUP_W
    cw = conv_w.astype(F32)
    pad = lambda vec: jnp.zeros((1, HEAD_W), F32).at[0, GDN_HEADS:2 * GDN_HEADS].set(vec.astype(F32))
    col = lambda g: pl.BlockSpec((tm, w), lambda i, g=g: (i, g))
    halo = lambda g: pl.BlockSpec((8, w), lambda i, g=g: (jnp.maximum(i * (tm // 8) - 1, 0), g))
    const = lambda shape: pl.BlockSpec(shape, lambda i: (0, 0))
    out = pl.BlockSpec((tm, w), lambda i: (i, 0))
    sm_spec = pl.BlockSpec((tm, HEAD_W), lambda i: (i, 0))
    return pl.pallas_call(
        functools.partial(_gdn_prep_body, tiles_per_seq=seq // tm),
        grid=(t // tm,),
        in_specs=[col(COL_BQ), col(COL_BK), col(COL_BV), halo(COL_BQ), halo(COL_BK), halo(COL_BV),
                  sm_spec, const((CONV_K, w)), const((CONV_K, w)), const((CONV_K, w)),
                  const((1, HEAD_W)), const((1, HEAD_W))],
        out_specs=[out, out, out, sm_spec],
        out_shape=[jax.ShapeDtypeStruct((t, w), F32)] * 3 + [jax.ShapeDtypeStruct((t, HEAD_W), F32)],
        compiler_params=_cparams(("parallel",)),
        name="gdn_prep",
    )(proj, proj, proj, proj, proj, proj, small, cw[:, 0:w], cw[:, w:2 * w], cw[:, 2 * w:3 * w],
      pad(a_log), pad(dt_bias))


def _gdn_body(q_ref, k_ref, v_ref, sm_ref, smt_ref, z_ref, nw_ref, o_ref, state_ref):
    c = q_ref.shape[0]
    heads = range(GDN_HEADS)

    @pl.when(pl.program_id(1) == 0)
    def _():
        state_ref[...] = jnp.zeros(state_ref.shape, F32)

    row = lax.broadcasted_iota(jnp.int32, (c, c), 0)
    col = lax.broadcasted_iota(jnp.int32, (c, c), 1)
    incl = row >= col
    strict = row > col
    xor_idx = row ^ col
    sm = sm_ref[...]
    gc_col = _hdot(jnp.where(incl, 1.0, 0.0).astype(F32), sm)
    gc_row = _hdot(smt_ref[...], jnp.where(row <= col, 1.0, 0.0).astype(F32))
    sl = [slice(h * HEAD_W, (h + 1) * HEAD_W) for h in heads]
    q = [q_ref[:, s] for s in sl]
    k = [k_ref[:, s] for s in sl]
    beta = [sm[:, h:h + 1] for h in heads]
    gcol = [gc_col[:, GDN_HEADS + h:GDN_HEADS + h + 1] for h in heads]
    decay = [jnp.where(incl, jnp.exp(jnp.minimum(gcol[h] - gc_row[GDN_HEADS + h:GDN_HEADS + h + 1, :], 0.0)), 0.0)
             for h in heads]
    kb = [k[h] * beta[h] for h in heads]
    l_mat = [jnp.where(strict, _bdot_nt(kb[h], k[h]) * decay[h], 0.0) for h in heads]
    a_qk = [_bdot_nt(q[h], k[h]) * decay[h] for h in heads]
    eg = [jnp.exp(gcol[h]) for h in heads]
    rhs = [jnp.concatenate([v_ref[:, sl[h]] * beta[h], kb[h] * eg[h]], axis=1) for h in heads]
    x = [jnp.where(row == col, 1.0, 0.0).astype(F32)] * GDN_HEADS
    b = 1
    while b < c:
        level = (xor_idx >= b) & (xor_idx < 2 * b)
        ex = [_bdot(jnp.where(level, l_mat[h], 0.0), x[h]) for h in heads]
        x = [x[h] - _bdot(x[h], ex[h]) for h in heads]
        b *= 2
    sol = [_bdot(x[h], rhs[h]) for h in heads]
    state = [state_ref[h] for h in heads]
    v_new = [sol[h][:, :HEAD_W] - _bdot(sol[h][:, HEAD_W:], state[h]) for h in heads]
    o = [_bdot(q[h] * eg[h], state[h]) + _bdot(a_qk[h], v_new[h]) for h in heads]
    nw = nw_ref[...]
    for h in heads:
        g_last = gcol[h][c - 1:c, :]
        k_dec = k[h] * jnp.exp(g_last - gcol[h])
        state_ref[h] = state[h] * jnp.exp(g_last) + _bdot_tn(k_dec, v_new[h])
        z = z_ref[:, sl[h]]
        o_ref[:, sl[h]] = (_rms(o[h], nw) * (z * jax.nn.sigmoid(z))).astype(o_ref.dtype)


def gdn_recurrence(gq, gk, gv, gsm, proj, gdn_norm_w, batch, seq, chunk=GDN_CHUNK):
    t = gq.shape[0]
    nc = seq // chunk
    w = GROUP_W
    gsm_t = gsm[:, :8].T
    blk = pl.BlockSpec((chunk, w), lambda b, c: (b * nc + c, 0))
    return pl.pallas_call(
        _gdn_body,
        grid=(batch, nc),
        in_specs=[blk, blk, blk,
                  pl.BlockSpec((chunk, HEAD_W), lambda b, c: (b * nc + c, 0)),
                  pl.BlockSpec((8, chunk), lambda b, c: (0, b * nc + c)),
                  pl.BlockSpec((chunk, w), lambda b, c: (b * nc + c, COL_BZ)),
                  pl.BlockSpec((1, HEAD_W), lambda b, c: (0, 0))],
        out_specs=blk,
        out_shape=jax.ShapeDtypeStruct((t, w), BF16),
        scratch_shapes=[pltpu.VMEM((GDN_HEADS, GDN_DK, HEAD_W), F32)],
        compiler_params=_cparams(("parallel", "arbitrary")),
        name="gdn_recurrence",
    )(gq, gk, gv, gsm, gsm_t, proj, gdn_norm_w.astype(F32).reshape(1, HEAD_W))


def _top2_route(logits):
    lane = lax.broadcasted_iota(jnp.int32, logits.shape, 1).astype(F32)
    big = float(logits.shape[1])
    m1 = jnp.max(logits, axis=-1, keepdims=True)
    i1 = jnp.min(jnp.where(logits == m1, lane, big), axis=-1, keepdims=True)
    rest = jnp.where(lane == i1, -jnp.inf, logits)
    m2 = jnp.max(rest, axis=-1, keepdims=True)
    i2 = jnp.min(jnp.where(rest == m2, lane, big), axis=-1, keepdims=True)
    e = jnp.exp(m2 - m1)
    g1 = 1.0 / (1.0 + e)
    g2 = e / (1.0 + e)
    return jnp.where(lane == 0, i1, jnp.where(lane == 1, i2, jnp.where(lane == 2, g1, jnp.where(lane == 3, g2, 0.0))))


def _mixer_out_body(*refs, with_router):
    if with_router:
        (oa_ref, ob_ref, ga_ref, gb_ref, x_ref, wa_ref, wb_ref, wo_ref, nw_ref, rw_ref, rb_ref,
         xo_ref, ho_ref, ro_ref) = refs
    else:
        (oa_ref, ob_ref, ga_ref, gb_ref, x_ref, wa_ref, wb_ref, wo_ref, nw_ref, xo_ref, ho_ref) = refs
    ya = jnp.dot(oa_ref[...], wa_ref[...], preferred_element_type=F32)
    yb = jnp.dot(ob_ref[...], wb_ref[...], preferred_element_type=F32)
    merged = jax.nn.sigmoid(ga_ref[...]) * ya + jax.nn.sigmoid(gb_ref[...]) * yb
    x_new = x_ref[...] + jnp.dot(merged.astype(BF16), wo_ref[...], preferred_element_type=F32)
    xo_ref[...] = x_new
    hn = _rms(x_new, nw_ref[...])
    ho_ref[...] = hn.astype(ho_ref.dtype)
    if with_router:
        ro_ref[...] = _top2_route(_hdot(hn, rw_ref[...]) + rb_ref[...])


def mixer_out(oa, ob, proj, x, w_a, w_b, w_o, norm_w, router=None, tm=512):
    t, d = x.shape
    w = GROUP_W
    row = lambda width, g=0: pl.BlockSpec((tm, width), lambda i, g=g: (i, g))
    const = lambda shape: pl.BlockSpec(shape, lambda i: (0, 0))
    in_specs = [row(w), row(w), row(d, COL_GA * w // d), row(d, COL_GB * w // d), row(d),
                const((w, d)), const((w, d)), const((d, d)), const((1, d))]
    args = [oa, ob, proj, proj, x, w_a, w_b, w_o, norm_w.astype(F32).reshape(1, d)]
    out_specs = [row(d), row(d)]
    out_shape = [jax.ShapeDtypeStruct((t, d), F32),
                 jax.ShapeDtypeStruct((t, d), F32 if router is not None else BF16)]
    if router is not None:
        rw, rb = router
        rw_pad = jnp.zeros((d, HEAD_W), F32).at[:, :N_EXPERTS].set(rw.astype(F32))
        rb_pad = jnp.full((1, HEAD_W), -jnp.inf, F32).at[0, :N_EXPERTS].set(rb.astype(F32))
        in_specs += [const((d, HEAD_W)), const((1, HEAD_W))]
        args += [rw_pad, rb_pad]
        out_specs.append(row(HEAD_W))
        out_shape.append(jax.ShapeDtypeStruct((t, HEAD_W), F32))
    return pl.pallas_call(
        functools.partial(_mixer_out_body, with_router=router is not None),
        grid=(t // tm,),
        in_specs=in_specs,
        out_specs=out_specs,
        out_shape=out_shape,
        compiler_params=_cparams(("parallel",)),
        name="mixer_out",
    )(*args)


def _ffn_body(*refs, with_norm, tc):
    if with_norm:
        h_ref, x_ref, wgu_ref, wd_ref, nw_ref, xo_ref, ho_ref, acc_ref = refs
    else:
        h_ref, x_ref, wgu_ref, wd_ref, xo_ref, acc_ref = refs
    h = h_ref[...]
    d_ff = wd_ref.shape[0]
    for ci, c0 in enumerate(range(0, d_ff, tc)):
        g = jnp.dot(h, wgu_ref[:, c0:c0 + tc], preferred_element_type=F32)
        u = jnp.dot(h, wgu_ref[:, d_ff + c0:d_ff + c0 + tc], preferred_element_type=F32)
        a = (g * jax.nn.sigmoid(g) * u).astype(BF16)
        part = jnp.dot(a, wd_ref[c0:c0 + tc, :], preferred_element_type=F32)
        if ci == 0:
            acc_ref[...] = x_ref[...] + part
        else:
            acc_ref[...] += part
    x_new = acc_ref[...]
    xo_ref[...] = x_new
    if with_norm:
        ho_ref[...] = _rms(x_new, nw_ref[...]).astype(ho_ref.dtype)


def dense_ffn(h, x, w_gu, w_d, next_norm_w=None, tm=512, tc=256):
    t, d = x.shape
    d_ff = w_d.shape[0]
    row = pl.BlockSpec((tm, d), lambda i: (i, 0))
    const = lambda shape: pl.BlockSpec(shape, lambda i: (0, 0), pipeline_mode=pl.Buffered(1))
    in_specs = [row, row, const((d, 2 * d_ff)), const((d_ff, d))]
    args = [h, x, w_gu, w_d]
    out_specs = [row]
    out_shape = [jax.ShapeDtypeStruct((t, d), F32)]
    if next_norm_w is not None:
        in_specs.append(pl.BlockSpec((1, d), lambda i: (0, 0)))
        args.append(next_norm_w.astype(F32).reshape(1, d))
        out_specs.append(row)
        out_shape.append(jax.ShapeDtypeStruct((t, d), BF16))
    return pl.pallas_call(
        functools.partial(_ffn_body, with_norm=next_norm_w is not None, tc=tc),
        grid=(t // tm,),
        in_specs=in_specs,
        out_specs=out_specs,
        out_shape=out_shape,
        scratch_shapes=[pltpu.VMEM((tm, d), F32)],
        compiler_params=_cparams(("parallel",)),
        name="dense_ffn",
    )(*args)


def _expert_body(be_ref, tok_ref, h_ref, wg_ref, wu_ref, wd_ref, o_ref, xbuf_ref, xb_ref, acc_ref, sem, *, n_c):
    del be_ref
    blk = pl.program_id(0)
    c = pl.program_id(1)
    n_blk = pl.num_programs(0)
    tm = xb_ref.shape[0]
    slot = blk % 2
    per_step = tm // n_c

    def row_copy(b, r, s):
        return pltpu.make_async_copy(h_ref.at[pl.ds(tok_ref[b * tm + r], 1)], xbuf_ref.at[s, pl.ds(r, 1)], sem.at[s])

    def wait_block(b, s):
        def body(r, carry):
            row_copy(b, r, s).wait()
            return carry
        lax.fori_loop(0, tm, body, 0)

    @pl.when(jnp.logical_and(blk == 0, c == 0))
    def _():
        def body(r, carry):
            row_copy(0, r, 0).start()
            return carry
        lax.fori_loop(0, tm, body, 0)

    @pl.when(c == 0)
    def _():
        wait_block(blk, slot)
        xb_ref[...] = xbuf_ref[slot].astype(BF16)

    nxt = jnp.where(blk + 1 < n_blk, blk + 1, 0)
    for i in range(per_step):
        row_copy(nxt, c * per_step + i, 1 - slot).start()

    xb = xb_ref[...]
    g = jnp.dot(xb, wg_ref[0], preferred_element_type=F32)
    u = jnp.dot(xb, wu_ref[0], preferred_element_type=F32)
    a = (g * jax.nn.sigmoid(g) * u).astype(BF16)
    part = jnp.dot(a, wd_ref[0], preferred_element_type=F32)

    @pl.when(c == 0)
    def _():
        acc_ref[...] = part

    @pl.when(c > 0)
    def _():
        acc_ref[...] += part

    @pl.when(c == n_c - 1)
    def _():
        o_ref[...] = acc_ref[...]

    @pl.when(jnp.logical_and(blk == n_blk - 1, c == n_c - 1))
    def _():
        wait_block(nxt, 1 - slot)


def moe_experts(h, slot_tok, block_e, w_gu, w_d, tm=MOE_TM, tc=MOE_TC):
    n_slots = slot_tok.shape[0]
    d = h.shape[1]
    d_e = w_d.shape[1]
    n_c = d_e // tc
    assert tm % n_c == 0, "each grid step gathers an equal share of the next block's rows"
    grid_spec = pltpu.PrefetchScalarGridSpec(
        num_scalar_prefetch=2,
        grid=(n_slots // tm, n_c),
        in_specs=[pl.BlockSpec(memory_space=pl.ANY),
                  pl.BlockSpec((1, d, tc), lambda b, c, be, tok: (be[b], 0, c)),
                  pl.BlockSpec((1, d, tc), lambda b, c, be, tok: (be[b], 0, n_c + c)),
                  pl.BlockSpec((1, tc, d), lambda b, c, be, tok: (be[b], c, 0))],
        out_specs=pl.BlockSpec((tm, d), lambda b, c, be, tok: (b, 0)),
        scratch_shapes=[pltpu.VMEM((2, tm, d), F32), pltpu.VMEM((tm, d), BF16), pltpu.VMEM((tm, d), F32),
                        pltpu.SemaphoreType.DMA((2,))],
    )
    return pl.pallas_call(
        functools.partial(_expert_body, n_c=n_c),
        grid_spec=grid_spec,
        out_shape=jax.ShapeDtypeStruct((n_slots, d), F32),
        compiler_params=_cparams(("arbitrary", "arbitrary")),
        name="moe_experts",
    )(block_e, slot_tok, h, w_gu, w_gu, w_d)


def _combine_body(dest_ref, x_ref, r_ref, yb_ref, o_ref, buf_ref, sem, *, tb):
    base = pl.program_id(0) * tb

    def copy(r, k):
        src = yb_ref.at[pl.ds(dest_ref[2 * (base + r) + k], 1)]
        return pltpu.make_async_copy(src, buf_ref.at[k, pl.ds(r, 1)], sem)

    def start(r, carry):
        copy(r, 0).start()
        copy(r, 1).start()
        return carry

    def wait(r, carry):
        copy(r, 0).wait()
        copy(r, 1).wait()
        return carry

    lax.fori_loop(0, tb, start, 0)
    lax.fori_loop(0, tb, wait, 0)
    route = r_ref[...]
    o_ref[...] = x_ref[...] + route[:, 2:3] * buf_ref[0] + route[:, 3:4] * buf_ref[1]


def moe_combine(x, route, yb, dest, tb=256):
    t, d = x.shape
    grid_spec = pltpu.PrefetchScalarGridSpec(
        num_scalar_prefetch=1,
        grid=(t // tb,),
        in_specs=[pl.BlockSpec((tb, d), lambda i, dest: (i, 0)),
                  pl.BlockSpec((tb, HEAD_W), lambda i, dest: (i, 0)),
                  pl.BlockSpec(memory_space=pl.ANY)],
        out_specs=pl.BlockSpec((tb, d), lambda i, dest: (i, 0)),
        scratch_shapes=[pltpu.VMEM((2, tb, d), F32), pltpu.SemaphoreType.DMA(())],
    )
    return pl.pallas_call(
        functools.partial(_combine_body, tb=tb),
        grid_spec=grid_spec,
        out_shape=jax.ShapeDtypeStruct((t, d), F32),
        compiler_params=_cparams(("arbitrary",)),
        name="moe_combine",
    )(dest, x, route, yb)


def moe_layer(hn, x, route, w_gu, w_d, tm=MOE_TM):
    t, d = x.shape
    n_assign = 2 * t
    e_flat = route[:, :2].astype(jnp.int32).reshape(n_assign)
    onehot = (e_flat[:, None] == jnp.arange(N_EXPERTS, dtype=jnp.int32)[None, :]).astype(jnp.int32)
    incl = jnp.cumsum(onehot, axis=0)
    rank = jnp.sum((incl - onehot) * onehot, axis=1)
    counts = incl[-1]
    padded = (counts + tm - 1) // tm * tm
    pad_end = jnp.cumsum(padded)
    pad_start = pad_end - padded
    dest = (pad_start[e_flat] + rank).astype(jnp.int32)
    n_blocks = -(-n_assign // tm) + N_EXPERTS
    block_start = jnp.arange(n_blocks, dtype=jnp.int32) * tm
    block_e = jnp.minimum(jnp.sum((pad_end[None, :] <= block_start[:, None]).astype(jnp.int32), axis=1),
                          N_EXPERTS - 1).astype(jnp.int32)
    slot_tok = jnp.zeros((n_blocks * tm,), jnp.int32).at[dest].set(jnp.arange(n_assign, dtype=jnp.int32) // 2)
    yb = moe_experts(hn, slot_tok, block_e, w_gu, w_d)
    return moe_combine(x, route, yb, dest)


def _rope_coefficients(positions):
    half = ROT_DIM // 2
    inv_freq = ROPE_THETA ** (-jnp.arange(0, ROT_DIM, 2, dtype=F32) / ROT_DIM)
    ang = positions.astype(F32).reshape(-1, 1) * inv_freq
    cos, sin = jnp.cos(ang), jnp.sin(ang)
    t = ang.shape[0]
    ones = jnp.ones((t, ATT_DIM - ROT_DIM), F32)
    zeros_h = jnp.zeros((t, half), F32)
    zeros_r = jnp.zeros((t, ATT_DIM - ROT_DIM), F32)
    c = jnp.concatenate([cos, cos, ones], axis=1)
    s1 = jnp.concatenate([-sin, zeros_h, zeros_r], axis=1)
    s2 = jnp.concatenate([zeros_h, sin, zeros_r], axis=1)
    return tuple(jnp.concatenate([a, a], axis=1) for a in (c, s1, s2))


def _split_w_in(w_in):
    att = 3 * GROUP_W
    conv = 3 * GROUP_W
    o_z = att + conv
    o_small = o_z + GROUP_W
    o_ga = o_small + 2 * GDN_HEADS
    d = w_in.shape[0]
    main = jnp.concatenate([w_in[:, o_ga:], w_in[:, :o_small]], axis=1).astype(BF16)
    small = jnp.zeros((d, HEAD_W), F32).at[:, :2 * GDN_HEADS].set(w_in[:, o_small:o_ga]).astype(BF16)
    return main, small


def kernel(x, positions, norm_mix_w, w_in, q_norm_w, k_norm_w, lam_vec, subln_w, conv_w, a_log, dt_bias,
           gdn_norm_w, w_branch_a, w_branch_b, w_out, norm_ffn_w, ffn_w_gate_up, ffn_w_down, router_w,
           router_b, moe_w_gate_up, moe_w_down):
    batch, seq, d = x.shape
    depth = w_in.shape[0]
    t = batch * seq
    xt = x.reshape(t, d).astype(F32)
    rope_c, rope_s1, rope_s2 = _rope_coefficients(positions)
    h = rmsnorm(xt, norm_mix_w[0].astype(F32))
    for layer in range(depth):
        lam_init = 0.8 - 0.6 * math.exp(-0.3 * layer)
        is_moe = layer % 2 == 1
        j = layer // 2
        w_main, w_small = _split_w_in(w_in[layer])
        proj, small = in_proj(h, w_main, w_small)
        qs, kk, vx = attn_prep(proj, rope_c, rope_s1, rope_s2, q_norm_w[layer], k_norm_w[layer])
        oa = flash_diff_attention(qs, kk, vx, q_norm_w[layer], k_norm_w[layer], lam_vec[layer], subln_w[layer],
                                  lam_init, batch, seq)
        gq, gk, gv, gsm = gdn_prep(proj, small, conv_w[layer], a_log[layer], dt_bias[layer], seq)
        ob = gdn_recurrence(gq, gk, gv, gsm, proj, gdn_norm_w[layer], batch, seq)
        router = (router_w[j], router_b[j]) if is_moe else None
        outs = mixer_out(oa, ob, proj, xt, w_branch_a[layer].astype(BF16), w_branch_b[layer].astype(BF16),
                         w_out[layer].astype(BF16), norm_ffn_w[layer], router)
        if is_moe:
            xt, hn, route = outs
            xt = moe_layer(hn, xt, route, moe_w_gate_up[j].astype(BF16), moe_w_down[j].astype(BF16))
            if layer + 1 < depth:
                h = rmsnorm(xt, norm_mix_w[layer + 1].astype(F32))
        else:
            xt, hn = outs
            nxt = norm_mix_w[layer + 1] if layer + 1 < depth else None
            res = dense_ffn(hn, xt, ffn_w_gate_up[j].astype(BF16), ffn_w_down[j].astype(BF16), nxt)
            if nxt is not None:
                xt, h = res
            else:
                xt = res[0]
    return xt.reshape(batch, seq, d)
```

```python
import functools
import math

import jax
import jax.numpy as jnp
from jax import lax
from jax.experimental import pallas as pl
from jax.experimental.pallas import tpu as pltpu

F32 = jnp.float32
BF16 = jnp.bfloat16

ATT_HEADS = 4
ATT_DIM = 64
ROT_DIM = ATT_DIM // 4
ROPE_THETA = 500000.0
GDN_HEADS = 4
GDN_DK = 128
CONV_K = 4
N_EXPERTS = 8
NORM_EPS = 1e-6

HEAD_W = 128
GROUP_W = 512
COL_GA, COL_GB, COL_AQ, COL_AK, COL_AV, COL_BQ, COL_BK, COL_BV, COL_BZ = 0, 2, 4, 5, 6, 7, 8, 9, 10
N_GROUPS = 11

VMEM_LIMIT = 56 * 1024 * 1024

GDN_CHUNK = 256
ATT_BLOCK = 1024
ATT_CHUNK = 256
LOG2E = math.log2(math.e)
BOUND_SLACK = 1.01
MAX_SAFE_LOG2_BOUND = 56.0
MOE_TM = 896
MOE_TC = 512
WAIT_RUN = 16


def _cparams(sem):
    return pltpu.CompilerParams(dimension_semantics=sem, vmem_limit_bytes=VMEM_LIMIT)


def _bdot(a, b):
    return jnp.dot(a.astype(BF16), b.astype(BF16), preferred_element_type=F32)


def _bdot_nt(a, b):
    return lax.dot_general(a.astype(BF16), b.astype(BF16), (((1,), (1,)), ((), ())),
                           preferred_element_type=F32)


def _bdot_tn(a, b):
    return lax.dot_general(a.astype(BF16), b.astype(BF16), (((0,), (0,)), ((), ())),
                           preferred_element_type=F32)


def _hdot(a, b):
    return jnp.dot(a, b, preferred_element_type=F32, precision=lax.Precision.HIGHEST)


def _rms(x, w):
    return x * lax.rsqrt(jnp.mean(x * x, axis=-1, keepdims=True) + NORM_EPS) * w


def _rmsnorm_body(x_ref, w_ref, o_ref):
    o_ref[...] = _rms(x_ref[...], w_ref[...]).astype(o_ref.dtype)


def rmsnorm(x, w, out_dtype=BF16, tm=1024):
    t, d = x.shape
    return pl.pallas_call(
        _rmsnorm_body,
        grid=(t // tm,),
        in_specs=[pl.BlockSpec((tm, d), lambda i: (i, 0)), pl.BlockSpec((1, d), lambda i: (0, 0))],
        out_specs=pl.BlockSpec((tm, d), lambda i: (i, 0)),
        out_shape=jax.ShapeDtypeStruct((t, d), out_dtype),
        compiler_params=_cparams(("parallel",)),
        name="rmsnorm",
    )(x, w.reshape(1, d))


def _in_proj_body(h_ref, w_ref, ws_ref, o_ref, os_ref):
    h = h_ref[...]
    o_ref[...] = jnp.dot(h, w_ref[...], preferred_element_type=F32)

    @pl.when(pl.program_id(1) == 0)
    def _():
        os_ref[...] = jnp.dot(h, ws_ref[...], preferred_element_type=F32)


def in_proj(h, w_main, w_small, tm=1024, n_col_steps=2):
    t, d = h.shape
    n = w_main.shape[1]
    tn = n // n_col_steps
    return pl.pallas_call(
        _in_proj_body,
        grid=(t // tm, n_col_steps),
        in_specs=[pl.BlockSpec((tm, d), lambda i, j: (i, 0)),
                  pl.BlockSpec((d, tn), lambda i, j: (0, j)),
                  pl.BlockSpec((d, HEAD_W), lambda i, j: (0, 0))],
        out_specs=[pl.BlockSpec((tm, tn), lambda i, j: (i, j)),
                   pl.BlockSpec((tm, HEAD_W), lambda i, j: (i, 0))],
        out_shape=[jax.ShapeDtypeStruct((t, n), F32), jax.ShapeDtypeStruct((t, HEAD_W), F32)],
        compiler_params=_cparams(("parallel", "arbitrary")),
        name="in_proj",
    )(h, w_main, w_small)


def _group_mean_sq(x, m_ref):
    ss = x * x
    hi = ss.astype(BF16)
    lo = (ss - hi.astype(F32)).astype(BF16)
    m = m_ref[...]
    gs = jnp.dot(hi, m, preferred_element_type=F32) + jnp.dot(lo, m, preferred_element_type=F32)
    return gs * (1.0 / ATT_DIM)


def _attn_prep_body(q_ref, k_ref, v_ref, c_ref, s1_ref, s2_ref, qw_ref, kw_ref, m_ref,
                    qs_ref, ko_ref, vx_ref):
    width = q_ref.shape[1]
    reps = width // HEAD_W
    c = jnp.concatenate([c_ref[...]] * reps, axis=1)
    s1 = jnp.concatenate([s1_ref[...]] * reps, axis=1)
    s2 = jnp.concatenate([s2_ref[...]] * reps, axis=1)

    def norm_rope(x, w):
        y = x * lax.rsqrt(_group_mean_sq(x, m_ref) + NORM_EPS) * w
        half = ROT_DIM // 2
        return y * c + pltpu.roll(y, width - half, 1) * s1 + pltpu.roll(y, half, 1) * s2

    q = norm_rope(q_ref[...], qw_ref[...]) * (ATT_DIM ** -0.5 * LOG2E)
    lane = lax.broadcasted_iota(jnp.int32, q.shape, 1)
    first_map = (lane % HEAD_W) < ATT_DIM
    qs_ref[0] = jnp.where(first_map, q, 0.0).astype(BF16)
    qs_ref[1] = jnp.where(first_map, 0.0, q).astype(BF16)
    ko_ref[...] = norm_rope(k_ref[...], kw_ref[...]).astype(BF16)
    v = v_ref[...].astype(BF16)
    ones = jnp.ones((v.shape[0], HEAD_W), BF16)
    parts = []
    for h in range(reps):
        parts += [v[:, h * HEAD_W:(h + 1) * HEAD_W], ones]
    vx_ref[...] = jnp.concatenate(parts, axis=1)


def attn_prep(proj, rope_c, rope_s1, rope_s2, q_norm_w, k_norm_w, tm=512):
    t = proj.shape[0]
    w = GROUP_W
    grp = jnp.arange(w) // ATT_DIM
    ones_bd = (grp[:, None] == grp[None, :]).astype(BF16)
    qw = jnp.tile(q_norm_w.astype(F32), w // ATT_DIM).reshape(1, w)
    kw = jnp.tile(k_norm_w.astype(F32), w // ATT_DIM).reshape(1, w)
    col = lambda g: pl.BlockSpec((tm, w), lambda i, g=g: (i, g))
    tab = pl.BlockSpec((tm, HEAD_W), lambda i: (i, 0))
    const = lambda shape: pl.BlockSpec(shape, lambda i: (0, 0))
    return pl.pallas_call(
        _attn_prep_body,
        grid=(t // tm,),
        in_specs=[col(COL_AQ), col(COL_AK), col(COL_AV), tab, tab, tab,
                  const((1, w)), const((1, w)), const((w, w))],
        out_specs=[pl.BlockSpec((2, tm, w), lambda i: (0, i, 0)), pl.BlockSpec((tm, w), lambda i: (i, 0)),
                   pl.BlockSpec((tm, 2 * w), lambda i: (i, 0))],
        out_shape=[jax.ShapeDtypeStruct((2, t, w), BF16), jax.ShapeDtypeStruct((t, w), BF16),
                   jax.ShapeDtypeStruct((t, 2 * w), BF16)],
        compiler_params=_cparams(("parallel",)),
        name="attn_prep",
    )(proj, proj, proj, rope_c, rope_s1, rope_s2, qw, kw, ones_bd)


def _flash_body(qi_ref, ki_ref, fast_ref, bound_ref, q_ref, k_ref, vx_ref, lam_ref, sw_ref, o_ref,
                m_ref, acc_ref, *, lam_init, chunk):
    p_idx = pl.program_id(2)
    qi = qi_ref[p_idx]
    ki = ki_ref[p_idx]
    tq = q_ref.shape[1]
    tk = k_ref.shape[0]
    fast = fast_ref[0] == 1
    diag = ki == qi

    @pl.when(ki == 0)
    def _():
        m_ref[...] = jnp.full(m_ref.shape, -jnp.inf, F32)
        acc_ref[...] = jnp.zeros(acc_ref.shape, F32)

    def keep_mask(c0, width):
        row = lax.broadcasted_iota(jnp.int32, (2 * tq, width), 0)
        col = lax.broadcasted_iota(jnp.int32, (2 * tq, width), 1) + c0
        return col <= jnp.where(row >= tq, row - tq, row)

    def update_bounded(masked):
        q2 = q_ref[...].reshape(2 * tq, HEAD_W)
        bound = bound_ref[0]
        for c0 in range(0, tk, chunk):
            s = lax.dot_general(q2, k_ref[c0:c0 + chunk, :], (((1,), (1,)), ((), ())),
                                preferred_element_type=F32)
            p = jnp.exp2(s - bound)
            if masked:
                p = jnp.where(keep_mask(c0, chunk), p, 0.0)
            acc_ref[...] += jnp.dot(p.astype(BF16), vx_ref[c0:c0 + chunk, :], preferred_element_type=F32)

    def update_running_max(masked):
        q2 = q_ref[...].reshape(2 * tq, HEAD_W)
        s = lax.dot_general(q2, k_ref[...], (((1,), (1,)), ((), ())), preferred_element_type=F32)
        if masked:
            s = jnp.where(keep_mask(0, tk), s, -jnp.inf)
        m_prev = m_ref[...]
        m_new = jnp.maximum(m_prev, jnp.max(s, axis=-1, keepdims=True))
        p = jnp.exp2(s - m_new)
        acc_ref[...] = (jnp.exp2(m_prev - m_new) * acc_ref[...]
                        + jnp.dot(p.astype(BF16), vx_ref[...], preferred_element_type=F32))
        m_ref[...] = m_new

    for use_fast, update in ((True, update_bounded), (False, update_running_max)):
        path = fast if use_fast else jnp.logical_not(fast)

        @pl.when(jnp.logical_and(path, jnp.logical_not(diag)))
        def _(update=update):
            update(False)

        @pl.when(jnp.logical_and(path, diag))
        def _(update=update):
            update(True)

    @pl.when(diag)
    def _():
        lv = lam_ref[...]
        lam = (jnp.exp(jnp.sum(lv[0:1] * lv[1:2], axis=-1, keepdims=True))
               - jnp.exp(jnp.sum(lv[2:3] * lv[3:4], axis=-1, keepdims=True)) + lam_init)
        o = (acc_ref[0:tq, 0:HEAD_W] / acc_ref[0:tq, HEAD_W:]
             - lam * (acc_ref[tq:, 0:HEAD_W] / acc_ref[tq:, HEAD_W:]))
        o_ref[...] = (_rms(o, sw_ref[...]) * (1.0 - lam_init)).astype(o_ref.dtype)


def flash_diff_attention(qs, k, vx, q_norm_w, k_norm_w, lam_vec, subln_w, lam_init, batch, seq,
                         blk=ATT_BLOCK, chunk=ATT_CHUNK):
    t = k.shape[0]
    nq = seq // blk
    pairs = [(i, j) for i in range(nq) for j in range(i + 1)]
    qi_tab = jnp.asarray([p[0] for p in pairs], jnp.int32)
    ki_tab = jnp.asarray([p[1] for p in pairs], jnp.int32)
    bound = (ATT_DIM ** 0.5) * jnp.max(jnp.abs(q_norm_w.astype(F32))) * jnp.max(jnp.abs(k_norm_w.astype(F32)))
    bound = bound * (LOG2E * BOUND_SLACK)
    fast = (bound <= MAX_SAFE_LOG2_BOUND).astype(jnp.int32).reshape(1)
    idx = lambda f: (lambda b, h, p, qi, ki, fast: f(b, h, qi[p], ki[p]))
    grid_spec = pltpu.PrefetchScalarGridSpec(
        num_scalar_prefetch=3,
        grid=(batch, ATT_HEADS, len(pairs)),
        in_specs=[pl.BlockSpec(memory_space=pltpu.SMEM),
                  pl.BlockSpec((2, blk, HEAD_W), idx(lambda b, h, qi, ki: (0, b * nq + qi, h))),
                  pl.BlockSpec((blk, HEAD_W), idx(lambda b, h, qi, ki: (b * nq + ki, h))),
                  pl.BlockSpec((blk, 2 * HEAD_W), idx(lambda b, h, qi, ki: (b * nq + ki, h))),
                  pl.BlockSpec((4, ATT_DIM), idx(lambda b, h, qi, ki: (0, 0))),
                  pl.BlockSpec((1, HEAD_W), idx(lambda b, h, qi, ki: (0, 0)))],
        out_specs=pl.BlockSpec((blk, HEAD_W), idx(lambda b, h, qi, ki: (b * nq + qi, h))),
        scratch_shapes=[pltpu.VMEM((2 * blk, 1), F32), pltpu.VMEM((2 * blk, 2 * HEAD_W), F32)],
    )
    return pl.pallas_call(
        functools.partial(_flash_body, lam_init=lam_init, chunk=chunk),
        grid_spec=grid_spec,
        out_shape=jax.ShapeDtypeStruct((t, ATT_HEADS * HEAD_W), BF16),
        compiler_params=_cparams(("parallel", "parallel", "arbitrary")),
        name="flash_diff_attention",
    )(qi_tab, ki_tab, fast, bound.reshape(1), qs, k, vx, lam_vec.astype(F32),
      subln_w.astype(F32).reshape(1, HEAD_W))


def _softplus(x):
    return jnp.maximum(x, 0.0) + jnp.log1p(jnp.exp(-jnp.abs(x)))


def _gdn_prep_body(q_ref, k_ref, v_ref, hq_ref, hk_ref, hv_ref, sm_ref, cq_ref, ck_ref, cv_ref,
                   al_ref, dt_ref, qo_ref, ko_ref, vo_ref, so_ref, *, tiles_per_seq):
    tm = q_ref.shape[0]
    seq_start = (pl.program_id(0) % tiles_per_seq) == 0

    def conv_silu(x_ref, halo_ref, w_ref):
        x = x_ref[...]
        halo = jnp.where(seq_start, 0.0, halo_ref[...])
        w = w_ref[...]
        head = x[0:8]
        row8 = lax.broadcasted_iota(jnp.int32, head.shape, 0)
        y = x * w[CONV_K - 1:CONV_K]
        y_head = head * w[CONV_K - 1:CONV_K]
        for j in range(1, CONV_K):
            wj = w[CONV_K - 1 - j:CONV_K - j]
            y = y + pltpu.roll(x, j, 0) * wj
            shifted = jnp.where(row8 < j, pltpu.roll(halo, j, 0), pltpu.roll(head, j, 0))
            y_head = y_head + shifted * wj
        y = jnp.concatenate([y_head, y[8:]], axis=0)
        return y * jax.nn.sigmoid(y)

    def l2n(x):
        parts = []
        for h in range(GDN_HEADS):
            xh = x[:, h * HEAD_W:(h + 1) * HEAD_W]
            parts.append(xh * lax.rsqrt(jnp.sum(xh * xh, axis=-1, keepdims=True) + NORM_EPS))
        return jnp.concatenate(parts, axis=1)

    qo_ref[...] = l2n(conv_silu(q_ref, hq_ref, cq_ref)) * (GDN_DK ** -0.5)
    ko_ref[...] = l2n(conv_silu(k_ref, hk_ref, ck_ref))
    vo_ref[...] = conv_silu(v_ref, hv_ref, cv_ref)
    sm = sm_ref[...]
    lane = lax.broadcasted_iota(jnp.int32, sm.shape, 1)
    beta = jax.nn.sigmoid(sm)
    g = -jnp.exp(al_ref[...]) * _softplus(sm + dt_ref[...])
    so_ref[...] = jnp.where(lane < GDN_HEADS, beta, jnp.where(lane < 2 * GDN_HEADS, g, 0.0))


def gdn_prep(proj, small, conv_w, a_log, dt_bias, seq, tm=512):
    t = proj.shape[0]
    w = GROUP_W
    cw = conv_w.astype(F32)
    pad = lambda vec: jnp.zeros((1, HEAD_W), F32).at[0, GDN_HEADS:2 * GDN_HEADS].set(vec.astype(F32))
    col = lambda g: pl.BlockSpec((tm, w), lambda i, g=g: (i, g))
    halo = lambda g: pl.BlockSpec((8, w), lambda i, g=g: (jnp.maximum(i * (tm // 8) - 1, 0), g))
    const = lambda shape: pl.BlockSpec(shape, lambda i: (0, 0))
    out = pl.BlockSpec((tm, w), lambda i: (i, 0))
    sm_spec = pl.BlockSpec((tm, HEAD_W), lambda i: (i, 0))
    return pl.pallas_call(
        functools.partial(_gdn_prep_body, tiles_per_seq=seq // tm),
        grid=(t // tm,),
        in_specs=[col(COL_BQ), col(COL_BK), col(COL_BV), halo(COL_BQ), halo(COL_BK), halo(COL_BV),
                  sm_spec, const((CONV_K, w)), const((CONV_K, w)), const((CONV_K, w)),
                  const((1, HEAD_W)), const((1, HEAD_W))],
        out_specs=[out, out, out, sm_spec],
        out_shape=[jax.ShapeDtypeStruct((t, w), F32)] * 3 + [jax.ShapeDtypeStruct((t, HEAD_W), F32)],
        compiler_params=_cparams(("parallel",)),
        name="gdn_prep",
    )(proj, proj, proj, proj, proj, proj, small, cw[:, 0:w], cw[:, w:2 * w], cw[:, 2 * w:3 * w],
      pad(a_log), pad(dt_bias))


def _gdn_body(q_ref, k_ref, v_ref, sm_ref, smt_ref, z_ref, nw_ref, o_ref, state_ref):
    c = q_ref.shape[0]
    heads = range(GDN_HEADS)

    @pl.when(pl.program_id(1) == 0)
    def _():
        state_ref[...] = jnp.zeros(state_ref.shape, F32)

    row = lax.broadcasted_iota(jnp.int32, (c, c), 0)
    col = lax.broadcasted_iota(jnp.int32, (c, c), 1)
    incl = row >= col
    strict = row > col
    xor_idx = row ^ col
    sm = sm_ref[...]
    gc_col = _hdot(jnp.where(incl, 1.0, 0.0).astype(F32), sm)
    gc_row = _hdot(smt_ref[...], jnp.where(row <= col, 1.0, 0.0).astype(F32))
    sl = [slice(h * HEAD_W, (h + 1) * HEAD_W) for h in heads]
    q = [q_ref[:, s] for s in sl]
    k = [k_ref[:, s] for s in sl]
    beta = [sm[:, h:h + 1] for h in heads]
    gcol = [gc_col[:, GDN_HEADS + h:GDN_HEADS + h + 1] for h in heads]
    decay = [jnp.where(incl, jnp.exp(jnp.minimum(gcol[h] - gc_row[GDN_HEADS + h:GDN_HEADS + h + 1, :], 0.0)), 0.0)
             for h in heads]
    kb = [k[h] * beta[h] for h in heads]
    l_mat = [jnp.where(strict, _bdot_nt(kb[h], k[h]) * decay[h], 0.0) for h in heads]
    a_qk = [_bdot_nt(q[h], k[h]) * decay[h] for h in heads]
    eg = [jnp.exp(gcol[h]) for h in heads]
    rhs = [jnp.concatenate([v_ref[:, sl[h]] * beta[h], kb[h] * eg[h]], axis=1) for h in heads]
    x = [jnp.where(row == col, 1.0, 0.0).astype(F32)] * GDN_HEADS
    b = 1
    while b < c:
        level = (xor_idx >= b) & (xor_idx < 2 * b)
        ex = [_bdot(jnp.where(level, l_mat[h], 0.0), x[h]) for h in heads]
        x = [x[h] - _bdot(x[h], ex[h]) for h in heads]
        b *= 2
    sol = [_bdot(x[h], rhs[h]) for h in heads]
    state = [state_ref[h] for h in heads]
    v_new = [sol[h][:, :HEAD_W] - _bdot(sol[h][:, HEAD_W:], state[h]) for h in heads]
    o = [_bdot(q[h] * eg[h], state[h]) + _bdot(a_qk[h], v_new[h]) for h in heads]
    nw = nw_ref[...]
    for h in heads:
        g_last = gcol[h][c - 1:c, :]
        k_dec = k[h] * jnp.exp(g_last - gcol[h])
        state_ref[h] = state[h] * jnp.exp(g_last) + _bdot_tn(k_dec, v_new[h])
        z = z_ref[:, sl[h]]
        o_ref[:, sl[h]] = (_rms(o[h], nw) * (z * jax.nn.sigmoid(z))).astype(o_ref.dtype)


def gdn_recurrence(gq, gk, gv, gsm, proj, gdn_norm_w, batch, seq, chunk=GDN_CHUNK):
    t = gq.shape[0]
    nc = seq // chunk
    w = GROUP_W
    gsm_t = gsm[:, :8].T
    blk = pl.BlockSpec((chunk, w), lambda b, c: (b * nc + c, 0))
    return pl.pallas_call(
        _gdn_body,
        grid=(batch, nc),
        in_specs=[blk, blk, blk,
                  pl.BlockSpec((chunk, HEAD_W), lambda b, c: (b * nc + c, 0)),
                  pl.BlockSpec((8, chunk), lambda b, c: (0, b * nc + c)),
                  pl.BlockSpec((chunk, w), lambda b, c: (b * nc + c, COL_BZ)),
                  pl.BlockSpec((1, HEAD_W), lambda b, c: (0, 0))],
        out_specs=blk,
        out_shape=jax.ShapeDtypeStruct((t, w), BF16),
        scratch_shapes=[pltpu.VMEM((GDN_HEADS, GDN_DK, HEAD_W), F32)],
        compiler_params=_cparams(("parallel", "arbitrary")),
        name="gdn_recurrence",
    )(gq, gk, gv, gsm, gsm_t, proj, gdn_norm_w.astype(F32).reshape(1, HEAD_W))


def _top2_route(logits):
    lane = lax.broadcasted_iota(jnp.int32, logits.shape, 1).astype(F32)
    big = float(logits.shape[1])
    m1 = jnp.max(logits, axis=-1, keepdims=True)
    i1 = jnp.min(jnp.where(logits == m1, lane, big), axis=-1, keepdims=True)
    rest = jnp.where(lane == i1, -jnp.inf, logits)
    m2 = jnp.max(rest, axis=-1, keepdims=True)
    i2 = jnp.min(jnp.where(rest == m2, lane, big), axis=-1, keepdims=True)
    e = jnp.exp(m2 - m1)
    g1 = 1.0 / (1.0 + e)
    g2 = e / (1.0 + e)
    return jnp.where(lane == 0, i1, jnp.where(lane == 1, i2, jnp.where(lane == 2, g1, jnp.where(lane == 3, g2, 0.0))))


def _mixer_out_body(*refs, with_router):
    if with_router:
        (oa_ref, ob_ref, ga_ref, gb_ref, x_ref, wa_ref, wb_ref, wo_ref, nw_ref, rwh_ref, rwl_ref, rb_ref,
         xo_ref, ho_ref, ro_ref) = refs
    else:
        (oa_ref, ob_ref, ga_ref, gb_ref, x_ref, wa_ref, wb_ref, wo_ref, nw_ref, xo_ref, ho_ref) = refs
    ya = jnp.dot(oa_ref[...], wa_ref[...], preferred_element_type=F32)
    yb = jnp.dot(ob_ref[...], wb_ref[...], preferred_element_type=F32)
    merged = jax.nn.sigmoid(ga_ref[...]) * ya + jax.nn.sigmoid(gb_ref[...]) * yb
    x_new = x_ref[...] + jnp.dot(merged.astype(BF16), wo_ref[...], preferred_element_type=F32)
    xo_ref[...] = x_new
    hn = _rms(x_new, nw_ref[...])
    ho_ref[...] = hn.astype(ho_ref.dtype)
    if with_router:
        hn_hi = hn.astype(BF16)
        hn_lo = (hn - hn_hi.astype(F32)).astype(BF16)
        rw_hi = rwh_ref[...]
        logits = (jnp.dot(hn_hi, rw_hi, preferred_element_type=F32)
                  + jnp.dot(hn_lo, rw_hi, preferred_element_type=F32)
                  + jnp.dot(hn_hi, rwl_ref[...], preferred_element_type=F32))
        ro_ref[...] = _top2_route(logits + rb_ref[...])


def mixer_out(oa, ob, proj, x, w_a, w_b, w_o, norm_w, router=None, tm=512):
    t, d = x.shape
    w = GROUP_W
    row = lambda width, g=0: pl.BlockSpec((tm, width), lambda i, g=g: (i, g))
    const = lambda shape: pl.BlockSpec(shape, lambda i: (0, 0))
    in_specs = [row(w), row(w), row(d, COL_GA * w // d), row(d, COL_GB * w // d), row(d),
                const((w, d)), const((w, d)), const((d, d)), const((1, d))]
    args = [oa, ob, proj, proj, x, w_a, w_b, w_o, norm_w.astype(F32).reshape(1, d)]
    out_specs = [row(d), row(d)]
    out_shape = [jax.ShapeDtypeStruct((t, d), F32),
                 jax.ShapeDtypeStruct((t, d), F32 if router is not None else BF16)]
    if router is not None:
        rw, rb = router
        rw_pad = jnp.zeros((d, HEAD_W), F32).at[:, :N_EXPERTS].set(rw.astype(F32))
        rw_hi = rw_pad.astype(BF16)
        rw_lo = (rw_pad - rw_hi.astype(F32)).astype(BF16)
        rb_pad = jnp.full((1, HEAD_W), -jnp.inf, F32).at[0, :N_EXPERTS].set(rb.astype(F32))
        in_specs += [const((d, HEAD_W)), const((d, HEAD_W)), const((1, HEAD_W))]
        args += [rw_hi, rw_lo, rb_pad]
        out_specs.append(row(HEAD_W))
        out_shape.append(jax.ShapeDtypeStruct((t, HEAD_W), F32))
    return pl.pallas_call(
        functools.partial(_mixer_out_body, with_router=router is not None),
        grid=(t // tm,),
        in_specs=in_specs,
        out_specs=out_specs,
        out_shape=out_shape,
        compiler_params=_cparams(("parallel",)),
        name="mixer_out",
    )(*args)


def _ffn_body(*refs, with_norm, tc):
    if with_norm:
        h_ref, x_ref, wgu_ref, wd_ref, nw_ref, xo_ref, ho_ref, acc_ref = refs
    else:
        h_ref, x_ref, wgu_ref, wd_ref, xo_ref, acc_ref = refs
    h = h_ref[...]
    d_ff = wd_ref.shape[0]
    for ci, c0 in enumerate(range(0, d_ff, tc)):
        g = jnp.dot(h, wgu_ref[:, c0:c0 + tc], preferred_element_type=F32)
        u = jnp.dot(h, wgu_ref[:, d_ff + c0:d_ff + c0 + tc], preferred_element_type=F32)
        a = (g * jax.nn.sigmoid(g) * u).astype(BF16)
        part = jnp.dot(a, wd_ref[c0:c0 + tc, :], preferred_element_type=F32)
        if ci == 0:
            acc_ref[...] = x_ref[...] + part
        else:
            acc_ref[...] += part
    x_new = acc_ref[...]
    xo_ref[...] = x_new
    if with_norm:
        ho_ref[...] = _rms(x_new, nw_ref[...]).astype(ho_ref.dtype)


def dense_ffn(h, x, w_gu, w_d, next_norm_w=None, tm=512, tc=256):
    t, d = x.shape
    d_ff = w_d.shape[0]
    row = pl.BlockSpec((tm, d), lambda i: (i, 0))
    const = lambda shape: pl.BlockSpec(shape, lambda i: (0, 0), pipeline_mode=pl.Buffered(1))
    in_specs = [row, row, const((d, 2 * d_ff)), const((d_ff, d))]
    args = [h, x, w_gu, w_d]
    out_specs = [row]
    out_shape = [jax.ShapeDtypeStruct((t, d), F32)]
    if next_norm_w is not None:
        in_specs.append(pl.BlockSpec((1, d), lambda i: (0, 0)))
        args.append(next_norm_w.astype(F32).reshape(1, d))
        out_specs.append(row)
        out_shape.append(jax.ShapeDtypeStruct((t, d), BF16))
    return pl.pallas_call(
        functools.partial(_ffn_body, with_norm=next_norm_w is not None, tc=tc),
        grid=(t // tm,),
        in_specs=in_specs,
        out_specs=out_specs,
        out_shape=out_shape,
        scratch_shapes=[pltpu.VMEM((tm, d), F32)],
        compiler_params=_cparams(("parallel",)),
        name="dense_ffn",
    )(*args)


def _expert_body(be_ref, tok_ref, nl_ref, h_ref, wg_ref, wu_ref, wd_ref, o_ref, xbuf_ref, xb_ref, acc_ref, sem,
                 *, n_c):
    del be_ref
    blk = pl.program_id(0)
    c = pl.program_id(1)
    n_blk = pl.num_programs(0)
    tm = xb_ref.shape[0]
    slot = blk % 2
    live = blk < nl_ref[0]
    per_step = tm // n_c

    def row_copy(b, r, s):
        return pltpu.make_async_copy(h_ref.at[pl.ds(tok_ref[b * tm + r], 1)], xbuf_ref.at[s, pl.ds(r, 1)], sem.at[s])

    def wait_block(s):
        def body(g, carry):
            for i in range(WAIT_RUN):
                r = g * WAIT_RUN + i
                pltpu.make_async_copy(h_ref.at[pl.ds(0, 1)], xbuf_ref.at[s, pl.ds(r, 1)], sem.at[s]).wait()
            return carry
        lax.fori_loop(0, tm // WAIT_RUN, body, 0)

    @pl.when(jnp.logical_and(blk == 0, c == 0))
    def _():
        def body(r, carry):
            row_copy(0, r, 0).start()
            return carry
        lax.fori_loop(0, tm, body, 0)

    @pl.when(c == 0)
    def _():
        wait_block(slot)
        xb_ref[...] = xbuf_ref[slot].astype(BF16)
        acc_ref[...] = jnp.zeros(acc_ref.shape, F32)

    nxt = jnp.where(blk + 1 < n_blk, blk + 1, 0)

    def gather_next():
        for i in range(per_step):
            row_copy(nxt, c * per_step + i, 1 - slot).start()

    @pl.when(live)
    def _():
        gather_next()
        xb = xb_ref[...]
        g = jnp.dot(xb, wg_ref[0], preferred_element_type=F32)
        u = jnp.dot(xb, wu_ref[0], preferred_element_type=F32)
        a = (g * jax.nn.sigmoid(g) * u).astype(BF16)
        acc_ref[...] += jnp.dot(a, wd_ref[0], preferred_element_type=F32)

    @pl.when(jnp.logical_not(live))
    def _():
        gather_next()

    @pl.when(c == n_c - 1)
    def _():
        o_ref[...] = acc_ref[...]

    @pl.when(jnp.logical_and(blk == n_blk - 1, c == n_c - 1))
    def _():
        wait_block(1 - slot)


def moe_experts(h, slot_tok, block_e, n_live, w_gu, w_d, tm=MOE_TM, tc=MOE_TC):
    n_slots = slot_tok.shape[0]
    d = h.shape[1]
    d_e = w_d.shape[1]
    n_c = d_e // tc
    assert tm % n_c == 0, "each grid step gathers an equal share of the next block's rows"
    assert tm % WAIT_RUN == 0
    grid_spec = pltpu.PrefetchScalarGridSpec(
        num_scalar_prefetch=3,
        grid=(n_slots // tm, n_c),
        in_specs=[pl.BlockSpec(memory_space=pl.ANY),
                  pl.BlockSpec((1, d, tc), lambda b, c, be, tok, nl: (be[b], 0, c)),
                  pl.BlockSpec((1, d, tc), lambda b, c, be, tok, nl: (be[b], 0, n_c + c)),
                  pl.BlockSpec((1, tc, d), lambda b, c, be, tok, nl: (be[b], c, 0))],
        out_specs=pl.BlockSpec((tm, d), lambda b, c, be, tok, nl: (b, 0)),
        scratch_shapes=[pltpu.VMEM((2, tm, d), F32), pltpu.VMEM((tm, d), BF16), pltpu.VMEM((tm, d), F32),
                        pltpu.SemaphoreType.DMA((2,))],
    )
    return pl.pallas_call(
        functools.partial(_expert_body, n_c=n_c),
        grid_spec=grid_spec,
        out_shape=jax.ShapeDtypeStruct((n_slots, d), F32),
        compiler_params=_cparams(("arbitrary", "arbitrary")),
        name="moe_experts",
    )(block_e, slot_tok, n_live, h, w_gu, w_gu, w_d)


def _combine_body(dest_ref, x_ref, r_ref, yb_ref, o_ref, buf_ref, sem, *, tb):
    step = pl.program_id(0)
    slot = step % 2

    def issue(st, s):
        def body(r, carry):
            for k in range(2):
                src = yb_ref.at[pl.ds(dest_ref[2 * (st * tb + r) + k], 1)]
                pltpu.make_async_copy(src, buf_ref.at[s, k, pl.ds(r, 1)], sem.at[s]).start()
            return carry
        lax.fori_loop(0, tb, body, 0, unroll=8)

    def wait_all(s):
        def body(g, carry):
            for i in range(WAIT_RUN):
                for k in range(2):
                    dst = buf_ref.at[s, k, pl.ds(g * WAIT_RUN + i, 1)]
                    pltpu.make_async_copy(yb_ref.at[pl.ds(0, 1)], dst, sem.at[s]).wait()
            return carry
        lax.fori_loop(0, tb // WAIT_RUN, body, 0)

    @pl.when(step == 0)
    def _():
        issue(0, 0)

    @pl.when(step + 1 < pl.num_programs(0))
    def _():
        issue(step + 1, 1 - slot)

    wait_all(slot)
    route = r_ref[...]
    o_ref[...] = x_ref[...] + route[:, 2:3] * buf_ref[slot, 0] + route[:, 3:4] * buf_ref[slot, 1]


def moe_combine(x, route, yb, dest, tb=256):
    t, d = x.shape
    assert tb % WAIT_RUN == 0
    grid_spec = pltpu.PrefetchScalarGridSpec(
        num_scalar_prefetch=1,
        grid=(t // tb,),
        in_specs=[pl.BlockSpec((tb, d), lambda i, dest: (i, 0)),
                  pl.BlockSpec((tb, HEAD_W), lambda i, dest: (i, 0)),
                  pl.BlockSpec(memory_space=pl.ANY)],
        out_specs=pl.BlockSpec((tb, d), lambda i, dest: (i, 0)),
        scratch_shapes=[pltpu.VMEM((2, 2, tb, d), F32), pltpu.SemaphoreType.DMA((2,))],
    )
    return pl.pallas_call(
        functools.partial(_combine_body, tb=tb),
        grid_spec=grid_spec,
        out_shape=jax.ShapeDtypeStruct((t, d), F32),
        compiler_params=_cparams(("arbitrary",)),
        name="moe_combine",
    )(dest, x, route, yb)


def moe_layer(hn, x, route, w_gu, w_d, tm=MOE_TM):
    t, d = x.shape
    n_assign = 2 * t
    e_flat = route[:, :2].astype(jnp.int32).reshape(n_assign)
    onehot = (e_flat[:, None] == jnp.arange(N_EXPERTS, dtype=jnp.int32)[None, :]).astype(jnp.int32)
    incl = jnp.cumsum(onehot, axis=0)
    rank = jnp.sum((incl - onehot) * onehot, axis=1)
    counts = incl[-1]
    padded = (counts + tm - 1) // tm * tm
    pad_end = jnp.cumsum(padded)
    pad_start = pad_end - padded
    dest = (pad_start[e_flat] + rank).astype(jnp.int32)
    n_blocks = -(-n_assign // tm) + N_EXPERTS
    block_start = jnp.arange(n_blocks, dtype=jnp.int32) * tm
    block_e = jnp.minimum(jnp.sum((pad_end[None, :] <= block_start[:, None]).astype(jnp.int32), axis=1),
                          N_EXPERTS - 1).astype(jnp.int32)
    slot_tok = jnp.zeros((n_blocks * tm,), jnp.int32).at[dest].set(jnp.arange(n_assign, dtype=jnp.int32) // 2,
                                                                  unique_indices=True)
    n_live = (pad_end[-1:] // tm).astype(jnp.int32)
    yb = moe_experts(hn, slot_tok, block_e, n_live, w_gu, w_d)
    return moe_combine(x, route, yb, dest)


def _rope_coefficients(positions):
    half = ROT_DIM // 2
    lane = jnp.arange(HEAD_W) % ATT_DIM
    inv_freq = ROPE_THETA ** (-(2.0 * (lane % half)).astype(F32) / ROT_DIM)
    freq = jnp.where(lane < ROT_DIM, inv_freq, 0.0)
    ang = positions.astype(F32).reshape(-1, 1) * freq[None, :]
    sin = jnp.sin(ang)
    return (jnp.cos(ang), jnp.where(lane < half, -sin, 0.0),
            jnp.where((lane >= half) & (lane < ROT_DIM), sin, 0.0))


def _split_w_in(w_in):
    att = 3 * GROUP_W
    conv = 3 * GROUP_W
    o_z = att + conv
    o_small = o_z + GROUP_W
    o_ga = o_small + 2 * GDN_HEADS
    d = w_in.shape[0]
    main = jnp.concatenate([w_in[:, o_ga:], w_in[:, :o_small]], axis=1).astype(BF16)
    small = jnp.zeros((d, HEAD_W), F32).at[:, :2 * GDN_HEADS].set(w_in[:, o_small:o_ga]).astype(BF16)
    return main, small


def kernel(x, positions, norm_mix_w, w_in, q_norm_w, k_norm_w, lam_vec, subln_w, conv_w, a_log, dt_bias,
           gdn_norm_w, w_branch_a, w_branch_b, w_out, norm_ffn_w, ffn_w_gate_up, ffn_w_down, router_w,
           router_b, moe_w_gate_up, moe_w_down):
    batch, seq, d = x.shape
    depth = w_in.shape[0]
    t = batch * seq
    xt = x.reshape(t, d).astype(F32)
    rope_c, rope_s1, rope_s2 = _rope_coefficients(positions)
    h = rmsnorm(xt, norm_mix_w[0].astype(F32))
    for layer in range(depth):
        lam_init = 0.8 - 0.6 * math.exp(-0.3 * layer)
        is_moe = layer % 2 == 1
        j = layer // 2
        w_main, w_small = _split_w_in(w_in[layer])
        proj, small = in_proj(h, w_main, w_small)
        qs, kk, vx = attn_prep(proj, rope_c, rope_s1, rope_s2, q_norm_w[layer], k_norm_w[layer])
        oa = flash_diff_attention(qs, kk, vx, q_norm_w[layer], k_norm_w[layer], lam_vec[layer], subln_w[layer],
                                  lam_init, batch, seq)
        gq, gk, gv, gsm = gdn_prep(proj, small, conv_w[layer], a_log[layer], dt_bias[layer], seq)
        ob = gdn_recurrence(gq, gk, gv, gsm, proj, gdn_norm_w[layer], batch, seq)
        router = (router_w[j], router_b[j]) if is_moe else None
        outs = mixer_out(oa, ob, proj, xt, w_branch_a[layer].astype(BF16), w_branch_b[layer].astype(BF16),
                         w_out[layer].astype(BF16), norm_ffn_w[layer], router)
        if is_moe:
            xt, hn, route = outs
            xt = moe_layer(hn, xt, route, moe_w_gate_up[j].astype(BF16), moe_w_down[j].astype(BF16))
            if layer + 1 < depth:
                h = rmsnorm(xt, norm_mix_w[layer + 1].astype(F32))
        else:
            xt, hn = outs
            nxt = norm_mix_w[layer + 1] if layer + 1 < depth else None
            res = dense_ffn(hn, xt, ffn_w_gate_up[j].astype(BF16), ffn_w_down[j].astype(BF16), nxt)
            if nxt is not None:
                xt, h = res
            else:
                xt = res[0]
    return xt.reshape(batch, seq, d)
```

```python
import functools
import math

import jax
import jax.numpy as jnp
from jax import lax
from jax.experimental import pallas as pl
from jax.experimental.pallas import tpu as pltpu

F32 = jnp.float32
BF16 = jnp.bfloat16

ATT_HEADS = 4
ATT_DIM = 64
ROT_DIM = ATT_DIM // 4
ROPE_THETA = 500000.0
GDN_HEADS = 4
GDN_DK = 128
CONV_K = 4
N_EXPERTS = 8
NORM_EPS = 1e-6

HEAD_W = 128
GROUP_W = 512
COL_GA, COL_GB, COL_AQ, COL_AK, COL_AV, COL_BQ, COL_BK, COL_BV, COL_BZ = 0, 2, 4, 5, 6, 7, 8, 9, 10
N_GROUPS = 11

VMEM_LIMIT = 56 * 1024 * 1024

GDN_CHUNK = 256
ATT_BLOCK = 1024
ATT_CHUNK = 256
LOG2E = math.log2(math.e)
BOUND_SLACK = 1.01
MAX_SAFE_LOG2_BOUND = 56.0
MOE_TM = 896
MOE_TC = 512
WAIT_RUN = 16
SUBLANES = 8


def _cparams(sem):
    return pltpu.CompilerParams(dimension_semantics=sem, vmem_limit_bytes=VMEM_LIMIT)


def _bdot(a, b):
    return jnp.dot(a.astype(BF16), b.astype(BF16), preferred_element_type=F32)


def _bdot_nt(a, b):
    return lax.dot_general(a.astype(BF16), b.astype(BF16), (((1,), (1,)), ((), ())),
                           preferred_element_type=F32)


def _bdot_tn(a, b):
    return lax.dot_general(a.astype(BF16), b.astype(BF16), (((0,), (0,)), ((), ())),
                           preferred_element_type=F32)


def _hdot(a, b):
    return jnp.dot(a, b, preferred_element_type=F32, precision=lax.Precision.HIGHEST)


def _rms(x, w):
    return x * lax.rsqrt(jnp.mean(x * x, axis=-1, keepdims=True) + NORM_EPS) * w


def _rmsnorm_body(x_ref, w_ref, o_ref):
    o_ref[...] = _rms(x_ref[...], w_ref[...]).astype(o_ref.dtype)


def rmsnorm(x, w, out_dtype=BF16, tm=1024):
    t, d = x.shape
    return pl.pallas_call(
        _rmsnorm_body,
        grid=(t // tm,),
        in_specs=[pl.BlockSpec((tm, d), lambda i: (i, 0)), pl.BlockSpec((1, d), lambda i: (0, 0))],
        out_specs=pl.BlockSpec((tm, d), lambda i: (i, 0)),
        out_shape=jax.ShapeDtypeStruct((t, d), out_dtype),
        compiler_params=_cparams(("parallel",)),
        name="rmsnorm",
    )(x, w.reshape(1, d))


def _in_proj_body(h_ref, w_ref, ws_ref, o_ref, os_ref):
    h = h_ref[...]
    o_ref[...] = jnp.dot(h, w_ref[...], preferred_element_type=F32)

    @pl.when(pl.program_id(1) == 0)
    def _():
        os_ref[...] = jnp.dot(h, ws_ref[...], preferred_element_type=F32)


def in_proj(h, w_main, w_small, tm=1024, n_col_steps=2):
    t, d = h.shape
    n = w_main.shape[1]
    tn = n // n_col_steps
    return pl.pallas_call(
        _in_proj_body,
        grid=(t // tm, n_col_steps),
        in_specs=[pl.BlockSpec((tm, d), lambda i, j: (i, 0)),
                  pl.BlockSpec((d, tn), lambda i, j: (0, j)),
                  pl.BlockSpec((d, HEAD_W), lambda i, j: (0, 0))],
        out_specs=[pl.BlockSpec((tm, tn), lambda i, j: (i, j)),
                   pl.BlockSpec((tm, HEAD_W), lambda i, j: (i, 0))],
        out_shape=[jax.ShapeDtypeStruct((t, n), F32), jax.ShapeDtypeStruct((t, HEAD_W), F32)],
        compiler_params=_cparams(("parallel", "arbitrary")),
        name="in_proj",
    )(h, w_main, w_small)


def _group_mean_sq(x, m_ref):
    ss = x * x
    hi = ss.astype(BF16)
    lo = (ss - hi.astype(F32)).astype(BF16)
    m = m_ref[...]
    gs = jnp.dot(hi, m, preferred_element_type=F32) + jnp.dot(lo, m, preferred_element_type=F32)
    return gs * (1.0 / ATT_DIM)


def _attn_prep_body(q_ref, k_ref, v_ref, c_ref, s1_ref, s2_ref, qw_ref, kw_ref, m_ref,
                    qs_ref, ko_ref, vx_ref):
    width = q_ref.shape[1]
    reps = width // HEAD_W
    c = jnp.concatenate([c_ref[...]] * reps, axis=1)
    s1 = jnp.concatenate([s1_ref[...]] * reps, axis=1)
    s2 = jnp.concatenate([s2_ref[...]] * reps, axis=1)

    def norm_rope(x, w):
        y = x * lax.rsqrt(_group_mean_sq(x, m_ref) + NORM_EPS) * w
        half = ROT_DIM // 2
        return y * c + pltpu.roll(y, width - half, 1) * s1 + pltpu.roll(y, half, 1) * s2

    q = norm_rope(q_ref[...], qw_ref[...]) * (ATT_DIM ** -0.5 * LOG2E)
    lane = lax.broadcasted_iota(jnp.int32, q.shape, 1)
    first_map = (lane % HEAD_W) < ATT_DIM
    qs_ref[0] = jnp.where(first_map, q, 0.0).astype(BF16)
    qs_ref[1] = jnp.where(first_map, 0.0, q).astype(BF16)
    ko_ref[...] = norm_rope(k_ref[...], kw_ref[...]).astype(BF16)
    v = v_ref[...].astype(BF16)
    ones = jnp.ones((v.shape[0], HEAD_W), BF16)
    parts = []
    for h in range(reps):
        parts += [v[:, h * HEAD_W:(h + 1) * HEAD_W], ones]
    vx_ref[...] = jnp.concatenate(parts, axis=1)


def attn_prep(proj, rope_c, rope_s1, rope_s2, q_norm_w, k_norm_w, tm=512):
    t = proj.shape[0]
    w = GROUP_W
    grp = jnp.arange(w) // ATT_DIM
    ones_bd = (grp[:, None] == grp[None, :]).astype(BF16)
    qw = jnp.tile(q_norm_w.astype(F32), w // ATT_DIM).reshape(1, w)
    kw = jnp.tile(k_norm_w.astype(F32), w // ATT_DIM).reshape(1, w)
    col = lambda g: pl.BlockSpec((tm, w), lambda i, g=g: (i, g))
    tab = pl.BlockSpec((tm, HEAD_W), lambda i: (i, 0))
    const = lambda shape: pl.BlockSpec(shape, lambda i: (0, 0))
    return pl.pallas_call(
        _attn_prep_body,
        grid=(t // tm,),
        in_specs=[col(COL_AQ), col(COL_AK), col(COL_AV), tab, tab, tab,
                  const((1, w)), const((1, w)), const((w, w))],
        out_specs=[pl.BlockSpec((2, tm, w), lambda i: (0, i, 0)), pl.BlockSpec((tm, w), lambda i: (i, 0)),
                   pl.BlockSpec((tm, 2 * w), lambda i: (i, 0))],
        out_shape=[jax.ShapeDtypeStruct((2, t, w), BF16), jax.ShapeDtypeStruct((t, w), BF16),
                   jax.ShapeDtypeStruct((t, 2 * w), BF16)],
        compiler_params=_cparams(("parallel",)),
        name="attn_prep",
    )(proj, proj, proj, rope_c, rope_s1, rope_s2, qw, kw, ones_bd)


def _flash_body(qi_ref, ki_ref, fast_ref, bound_ref, q_ref, k_ref, vx_ref, lam_ref, sw_ref, o_ref,
                m_ref, acc_ref, *, lam_init, chunk):
    p_idx = pl.program_id(1)
    qi = qi_ref[p_idx]
    ki = ki_ref[p_idx]
    tq = q_ref.shape[1]
    tk = k_ref.shape[0]
    fast = fast_ref[0] == 1
    diag = ki == qi

    @pl.when(ki == 0)
    def _():
        m_ref[...] = jnp.full(m_ref.shape, -jnp.inf, F32)
        acc_ref[...] = jnp.zeros(acc_ref.shape, F32)

    def keep_mask(c0, width):
        row = lax.broadcasted_iota(jnp.int32, (2 * tq, width), 0)
        col = lax.broadcasted_iota(jnp.int32, (2 * tq, width), 1) + c0
        return col <= jnp.where(row >= tq, row - tq, row)

    def head_operands(h):
        qk_lanes = pl.ds(pl.multiple_of(h * HEAD_W, HEAD_W), HEAD_W)
        v_lanes = pl.ds(pl.multiple_of(h * 2 * HEAD_W, 2 * HEAD_W), 2 * HEAD_W)
        return q_ref[:, :, qk_lanes].reshape(2 * tq, HEAD_W), qk_lanes, v_lanes

    def update_bounded(h, masked):
        q2, qk_lanes, v_lanes = head_operands(h)
        bound = bound_ref[0]
        for c0 in range(0, tk, chunk):
            s = lax.dot_general(q2, k_ref[c0:c0 + chunk, qk_lanes], (((1,), (1,)), ((), ())),
                                preferred_element_type=F32)
            p = jnp.exp2(s - bound)
            if masked:
                p = jnp.where(keep_mask(c0, chunk), p, 0.0)
            acc_ref[h] += jnp.dot(p.astype(BF16), vx_ref[c0:c0 + chunk, v_lanes], preferred_element_type=F32)

    def update_running_max(h, masked):
        q2, qk_lanes, v_lanes = head_operands(h)
        s = lax.dot_general(q2, k_ref[:, qk_lanes], (((1,), (1,)), ((), ())), preferred_element_type=F32)
        if masked:
            s = jnp.where(keep_mask(0, tk), s, -jnp.inf)
        m_prev = m_ref[h]
        m_new = jnp.maximum(m_prev, jnp.max(s, axis=-1, keepdims=True))
        p = jnp.exp2(s - m_new)
        acc_ref[h] = (jnp.exp2(m_prev - m_new) * acc_ref[h]
                      + jnp.dot(p.astype(BF16), vx_ref[:, v_lanes], preferred_element_type=F32))
        m_ref[h] = m_new

    def all_heads(update, masked):
        def body(h, carry):
            update(h, masked)
            return carry
        lax.fori_loop(0, ATT_HEADS, body, 0)

    for use_fast, update in ((True, update_bounded), (False, update_running_max)):
        path = fast if use_fast else jnp.logical_not(fast)

        @pl.when(jnp.logical_and(path, jnp.logical_not(diag)))
        def _(update=update):
            all_heads(update, False)

        @pl.when(jnp.logical_and(path, diag))
        def _(update=update):
            all_heads(update, True)

    @pl.when(diag)
    def _():
        lv = lam_ref[...]
        lam = (jnp.exp(jnp.sum(lv[0:1] * lv[1:2], axis=-1, keepdims=True))
               - jnp.exp(jnp.sum(lv[2:3] * lv[3:4], axis=-1, keepdims=True)) + lam_init)
        for h in range(ATT_HEADS):
            o = (acc_ref[h, 0:tq, 0:HEAD_W] / acc_ref[h, 0:tq, HEAD_W:]
                 - lam * (acc_ref[h, tq:, 0:HEAD_W] / acc_ref[h, tq:, HEAD_W:]))
            o_ref[:, h * HEAD_W:(h + 1) * HEAD_W] = (_rms(o, sw_ref[...]) * (1.0 - lam_init)).astype(o_ref.dtype)


def flash_diff_attention(qs, k, vx, q_norm_w, k_norm_w, lam_vec, subln_w, lam_init, batch, seq,
                         blk=ATT_BLOCK, chunk=ATT_CHUNK):
    t = k.shape[0]
    nq = seq // blk
    w = ATT_HEADS * HEAD_W
    pairs = [(i, j) for i in range(nq) for j in range(i + 1)]
    qi_tab = jnp.asarray([p[0] for p in pairs], jnp.int32)
    ki_tab = jnp.asarray([p[1] for p in pairs], jnp.int32)
    bound = (ATT_DIM ** 0.5) * jnp.max(jnp.abs(q_norm_w.astype(F32))) * jnp.max(jnp.abs(k_norm_w.astype(F32)))
    bound = bound * (LOG2E * BOUND_SLACK)
    fast = (bound <= MAX_SAFE_LOG2_BOUND).astype(jnp.int32).reshape(1)
    idx = lambda f: (lambda b, p, qi, ki, fast: f(b, qi[p], ki[p]))
    grid_spec = pltpu.PrefetchScalarGridSpec(
        num_scalar_prefetch=3,
        grid=(batch, len(pairs)),
        in_specs=[pl.BlockSpec(memory_space=pltpu.SMEM),
                  pl.BlockSpec((2, blk, w), idx(lambda b, qi, ki: (0, b * nq + qi, 0))),
                  pl.BlockSpec((blk, w), idx(lambda b, qi, ki: (b * nq + ki, 0))),
                  pl.BlockSpec((blk, 2 * w), idx(lambda b, qi, ki: (b * nq + ki, 0))),
                  pl.BlockSpec((4, ATT_DIM), idx(lambda b, qi, ki: (0, 0))),
                  pl.BlockSpec((1, HEAD_W), idx(lambda b, qi, ki: (0, 0)))],
        out_specs=pl.BlockSpec((blk, w), idx(lambda b, qi, ki: (b * nq + qi, 0))),
        scratch_shapes=[pltpu.VMEM((ATT_HEADS, 2 * blk, 1), F32),
                        pltpu.VMEM((ATT_HEADS, 2 * blk, 2 * HEAD_W), F32)],
    )
    return pl.pallas_call(
        functools.partial(_flash_body, lam_init=lam_init, chunk=chunk),
        grid_spec=grid_spec,
        out_shape=jax.ShapeDtypeStruct((t, w), BF16),
        compiler_params=_cparams(("parallel", "arbitrary")),
        name="flash_diff_attention",
    )(qi_tab, ki_tab, fast, bound.reshape(1), qs, k, vx, lam_vec.astype(F32),
      subln_w.astype(F32).reshape(1, HEAD_W))


def _softplus(x):
    return jnp.maximum(x, 0.0) + jnp.log1p(jnp.exp(-jnp.abs(x)))


def _conv_silu(x, halo, w):
    head = x[0:8]
    row8 = lax.broadcasted_iota(jnp.int32, head.shape, 0)
    y = x * w[CONV_K - 1:CONV_K]
    y_head = head * w[CONV_K - 1:CONV_K]
    for j in range(1, CONV_K):
        wj = w[CONV_K - 1 - j:CONV_K - j]
        y = y + pltpu.roll(x, j, 0) * wj
        shifted = jnp.where(row8 < j, pltpu.roll(halo, j, 0), pltpu.roll(head, j, 0))
        y_head = y_head + shifted * wj
    y = jnp.concatenate([y_head, y[8:]], axis=0)
    return y * jax.nn.sigmoid(y)


def _l2n(x):
    return x * lax.rsqrt(jnp.sum(x * x, axis=-1, keepdims=True) + NORM_EPS)


def _gdn_body(q_ref, k_ref, v_ref, hq_ref, hk_ref, hv_ref, sm_ref, cq_ref, ck_ref, cv_ref, al_ref, dt_ref,
              z_ref, nw_ref, o_ref, state_ref):
    c = q_ref.shape[0]
    heads = range(GDN_HEADS)
    seq_start = pl.program_id(1) == 0

    @pl.when(seq_start)
    def _():
        state_ref[...] = jnp.zeros(state_ref.shape, F32)

    def conv(x_ref, halo_ref, w_ref):
        return _conv_silu(x_ref[...], jnp.where(seq_start, 0.0, halo_ref[...]), w_ref[...])

    sl = [slice(h * HEAD_W, (h + 1) * HEAD_W) for h in heads]
    q_all = conv(q_ref, hq_ref, cq_ref)
    k_all = conv(k_ref, hk_ref, ck_ref)
    v_all = conv(v_ref, hv_ref, cv_ref)
    q = [_l2n(q_all[:, s]) * (GDN_DK ** -0.5) for s in sl]
    k = [_l2n(k_all[:, s]) for s in sl]
    raw = sm_ref[...]
    lane = lax.broadcasted_iota(jnp.int32, raw.shape, 1)
    sm = jnp.where(lane < GDN_HEADS, jax.nn.sigmoid(raw),
                   jnp.where(lane < 2 * GDN_HEADS, -jnp.exp(al_ref[...]) * _softplus(raw + dt_ref[...]), 0.0))

    row = lax.broadcasted_iota(jnp.int32, (c, c), 0)
    col = lax.broadcasted_iota(jnp.int32, (c, c), 1)
    incl = row >= col
    strict = row > col
    xor_idx = row ^ col
    gc_col = _hdot(jnp.where(incl, 1.0, 0.0).astype(F32), sm)
    gc_row = _hdot(sm.T[0:8], jnp.where(row <= col, 1.0, 0.0).astype(F32))
    beta = [sm[:, h:h + 1] for h in heads]
    gcol = [gc_col[:, GDN_HEADS + h:GDN_HEADS + h + 1] for h in heads]
    decay = [jnp.where(incl, jnp.exp(jnp.minimum(gcol[h] - gc_row[GDN_HEADS + h:GDN_HEADS + h + 1, :], 0.0)), 0.0)
             for h in heads]
    kb = [k[h] * beta[h] for h in heads]
    l_mat = [jnp.where(strict, _bdot_nt(kb[h], k[h]) * decay[h], 0.0) for h in heads]
    a_qk = [_bdot_nt(q[h], k[h]) * decay[h] for h in heads]
    eg = [jnp.exp(gcol[h]) for h in heads]
    rhs = [jnp.concatenate([v_all[:, sl[h]] * beta[h], kb[h] * eg[h]], axis=1) for h in heads]
    x = [jnp.where(row == col, 1.0, 0.0).astype(F32)] * GDN_HEADS
    b = 1
    while b < c:
        level = (xor_idx >= b) & (xor_idx < 2 * b)
        ex = [_bdot(jnp.where(level, l_mat[h], 0.0), x[h]) for h in heads]
        x = [x[h] - _bdot(x[h], ex[h]) for h in heads]
        b *= 2
    sol = [_bdot(x[h], rhs[h]) for h in heads]
    state = [state_ref[h] for h in heads]
    v_new = [sol[h][:, :HEAD_W] - _bdot(sol[h][:, HEAD_W:], state[h]) for h in heads]
    o = [_bdot(q[h] * eg[h], state[h]) + _bdot(a_qk[h], v_new[h]) for h in heads]
    nw = nw_ref[...]
    for h in heads:
        g_last = gcol[h][c - 1:c, :]
        k_dec = k[h] * jnp.exp(g_last - gcol[h])
        state_ref[h] = state[h] * jnp.exp(g_last) + _bdot_tn(k_dec, v_new[h])
        z = z_ref[:, sl[h]]
        o_ref[:, sl[h]] = (_rms(o[h], nw) * (z * jax.nn.sigmoid(z))).astype(o_ref.dtype)


def gated_delta_net(proj, small, conv_w, a_log, dt_bias, gdn_norm_w, batch, seq, chunk=GDN_CHUNK):
    t = proj.shape[0]
    nc = seq // chunk
    w = GROUP_W
    cw = conv_w.astype(F32)
    pad = lambda vec: jnp.zeros((1, HEAD_W), F32).at[0, GDN_HEADS:2 * GDN_HEADS].set(vec.astype(F32))
    col = lambda g: pl.BlockSpec((chunk, w), lambda b, c, g=g: (b * nc + c, g))
    halo = lambda g: pl.BlockSpec((8, w), lambda b, c, g=g: (jnp.maximum((b * nc + c) * (chunk // 8) - 1, 0), g))
    const = lambda shape: pl.BlockSpec(shape, lambda b, c: (0, 0))
    return pl.pallas_call(
        _gdn_body,
        grid=(batch, nc),
        in_specs=[col(COL_BQ), col(COL_BK), col(COL_BV), halo(COL_BQ), halo(COL_BK), halo(COL_BV),
                  pl.BlockSpec((chunk, HEAD_W), lambda b, c: (b * nc + c, 0)),
                  const((CONV_K, w)), const((CONV_K, w)), const((CONV_K, w)),
                  const((1, HEAD_W)), const((1, HEAD_W)), col(COL_BZ), const((1, HEAD_W))],
        out_specs=pl.BlockSpec((chunk, w), lambda b, c: (b * nc + c, 0)),
        out_shape=jax.ShapeDtypeStruct((t, w), BF16),
        scratch_shapes=[pltpu.VMEM((GDN_HEADS, GDN_DK, HEAD_W), F32)],
        compiler_params=_cparams(("parallel", "arbitrary")),
        name="gated_delta_net",
    )(proj, proj, proj, proj, proj, proj, small, cw[:, 0:w], cw[:, w:2 * w], cw[:, 2 * w:3 * w],
      pad(a_log), pad(dt_bias), proj, gdn_norm_w.astype(F32).reshape(1, HEAD_W))


def _top2_route(logits):
    lane = lax.broadcasted_iota(jnp.int32, logits.shape, 1).astype(F32)
    big = float(logits.shape[1])
    m1 = jnp.max(logits, axis=-1, keepdims=True)
    i1 = jnp.min(jnp.where(logits == m1, lane, big), axis=-1, keepdims=True)
    rest = jnp.where(lane == i1, -jnp.inf, logits)
    m2 = jnp.max(rest, axis=-1, keepdims=True)
    i2 = jnp.min(jnp.where(rest == m2, lane, big), axis=-1, keepdims=True)
    e = jnp.exp(m2 - m1)
    g1 = 1.0 / (1.0 + e)
    g2 = e / (1.0 + e)
    return jnp.where(lane == 0, i1, jnp.where(lane == 1, i2, jnp.where(lane == 2, g1, jnp.where(lane == 3, g2, 0.0))))


def _mixer_out_body(*refs, with_router):
    if with_router:
        (oa_ref, ob_ref, ga_ref, gb_ref, x_ref, wa_ref, wb_ref, wo_ref, nw_ref, rwh_ref, rwl_ref, rb_ref,
         xo_ref, ho_ref, ro_ref) = refs
    else:
        (oa_ref, ob_ref, ga_ref, gb_ref, x_ref, wa_ref, wb_ref, wo_ref, nw_ref, xo_ref, ho_ref) = refs
    ya = jnp.dot(oa_ref[...], wa_ref[...], preferred_element_type=F32)
    yb = jnp.dot(ob_ref[...], wb_ref[...], preferred_element_type=F32)
    merged = jax.nn.sigmoid(ga_ref[...]) * ya + jax.nn.sigmoid(gb_ref[...]) * yb
    x_new = x_ref[...] + jnp.dot(merged.astype(BF16), wo_ref[...], preferred_element_type=F32)
    xo_ref[...] = x_new
    hn = _rms(x_new, nw_ref[...])
    ho_ref[...] = hn.astype(ho_ref.dtype)
    if with_router:
        hn_hi = hn.astype(BF16)
        hn_lo = (hn - hn_hi.astype(F32)).astype(BF16)
        rw_hi = rwh_ref[...]
        logits = (jnp.dot(hn_hi, rw_hi, preferred_element_type=F32)
                  + jnp.dot(hn_lo, rw_hi, preferred_element_type=F32)
                  + jnp.dot(hn_hi, rwl_ref[...], preferred_element_type=F32))
        ro_ref[...] = _top2_route(logits + rb_ref[...])


def mixer_out(oa, ob, proj, x, w_a, w_b, w_o, norm_w, router=None, tm=512):
    t, d = x.shape
    w = GROUP_W
    row = lambda width, g=0: pl.BlockSpec((tm, width), lambda i, g=g: (i, g))
    const = lambda shape: pl.BlockSpec(shape, lambda i: (0, 0))
    in_specs = [row(w), row(w), row(d, COL_GA * w // d), row(d, COL_GB * w // d), row(d),
                const((w, d)), const((w, d)), const((d, d)), const((1, d))]
    args = [oa, ob, proj, proj, x, w_a, w_b, w_o, norm_w.astype(F32).reshape(1, d)]
    out_specs = [row(d), row(d)]
    out_shape = [jax.ShapeDtypeStruct((t, d), F32),
                 jax.ShapeDtypeStruct((t, d), F32 if router is not None else BF16)]
    if router is not None:
        rw, rb = router
        rw_pad = jnp.zeros((d, HEAD_W), F32).at[:, :N_EXPERTS].set(rw.astype(F32))
        rw_hi = rw_pad.astype(BF16)
        rw_lo = (rw_pad - rw_hi.astype(F32)).astype(BF16)
        rb_pad = jnp.full((1, HEAD_W), -jnp.inf, F32).at[0, :N_EXPERTS].set(rb.astype(F32))
        in_specs += [const((d, HEAD_W)), const((d, HEAD_W)), const((1, HEAD_W))]
        args += [rw_hi, rw_lo, rb_pad]
        out_specs.append(row(HEAD_W))
        out_shape.append(jax.ShapeDtypeStruct((t, HEAD_W), F32))
    return pl.pallas_call(
        functools.partial(_mixer_out_body, with_router=router is not None),
        grid=(t // tm,),
        in_specs=in_specs,
        out_specs=out_specs,
        out_shape=out_shape,
        compiler_params=_cparams(("parallel",)),
        name="mixer_out",
    )(*args)


def _ffn_body(*refs, with_norm, tc):
    if with_norm:
        h_ref, x_ref, wgu_ref, wd_ref, nw_ref, xo_ref, ho_ref, acc_ref = refs
    else:
        h_ref, x_ref, wgu_ref, wd_ref, xo_ref, acc_ref = refs
    h = h_ref[...]
    d_ff = wd_ref.shape[0]
    for ci, c0 in enumerate(range(0, d_ff, tc)):
        g = jnp.dot(h, wgu_ref[:, c0:c0 + tc], preferred_element_type=F32)
        u = jnp.dot(h, wgu_ref[:, d_ff + c0:d_ff + c0 + tc], preferred_element_type=F32)
        a = (g * jax.nn.sigmoid(g) * u).astype(BF16)
        part = jnp.dot(a, wd_ref[c0:c0 + tc, :], preferred_element_type=F32)
        if ci == 0:
            acc_ref[...] = x_ref[...] + part
        else:
            acc_ref[...] += part
    x_new = acc_ref[...]
    xo_ref[...] = x_new
    if with_norm:
        ho_ref[...] = _rms(x_new, nw_ref[...]).astype(ho_ref.dtype)


def dense_ffn(h, x, w_gu, w_d, next_norm_w=None, tm=512, tc=256):
    t, d = x.shape
    d_ff = w_d.shape[0]
    row = pl.BlockSpec((tm, d), lambda i: (i, 0))
    const = lambda shape: pl.BlockSpec(shape, lambda i: (0, 0), pipeline_mode=pl.Buffered(1))
    in_specs = [row, row, const((d, 2 * d_ff)), const((d_ff, d))]
    args = [h, x, w_gu, w_d]
    out_specs = [row]
    out_shape = [jax.ShapeDtypeStruct((t, d), F32)]
    if next_norm_w is not None:
        in_specs.append(pl.BlockSpec((1, d), lambda i: (0, 0)))
        args.append(next_norm_w.astype(F32).reshape(1, d))
        out_specs.append(row)
        out_shape.append(jax.ShapeDtypeStruct((t, d), BF16))
    return pl.pallas_call(
        functools.partial(_ffn_body, with_norm=next_norm_w is not None, tc=tc),
        grid=(t // tm,),
        in_specs=in_specs,
        out_specs=out_specs,
        out_shape=out_shape,
        scratch_shapes=[pltpu.VMEM((tm, d), F32)],
        compiler_params=_cparams(("parallel",)),
        name="dense_ffn",
    )(*args)


def _expert_body(be_ref, tok_ref, nl_ref, h_ref, wg_ref, wu_ref, wd_ref, o_ref, xbuf_ref, xb_ref, acc_ref, sem,
                 *, n_c):
    del be_ref
    blk = pl.program_id(0)
    c = pl.program_id(1)
    n_blk = pl.num_programs(0)
    tm = xb_ref.shape[0]
    slot = blk % 2
    live = blk < nl_ref[0]
    per_step = tm // n_c

    def row_copy(b, grp, i, s):
        tok = tok_ref[b * tm + grp * SUBLANES + i]
        return pltpu.make_async_copy(h_ref.at[pl.ds(tok, 1)], xbuf_ref.at[s, grp, pl.ds(i, 1)], sem.at[s])

    def wait_block(s):
        def body(g, carry):
            for j in range(WAIT_RUN // SUBLANES):
                for i in range(SUBLANES):
                    dst = xbuf_ref.at[s, g * (WAIT_RUN // SUBLANES) + j, pl.ds(i, 1)]
                    pltpu.make_async_copy(h_ref.at[pl.ds(0, 1)], dst, sem.at[s]).wait()
            return carry
        lax.fori_loop(0, tm // WAIT_RUN, body, 0)

    @pl.when(jnp.logical_and(blk == 0, c == 0))
    def _():
        def body(grp, carry):
            for i in range(SUBLANES):
                row_copy(0, grp, i, 0).start()
            return carry
        lax.fori_loop(0, tm // SUBLANES, body, 0)

    @pl.when(c == 0)
    def _():
        wait_block(slot)
        xb_ref[...] = xbuf_ref[slot].reshape(tm, xb_ref.shape[1]).astype(BF16)
        acc_ref[...] = jnp.zeros(acc_ref.shape, F32)

    nxt = jnp.where(blk + 1 < n_blk, blk + 1, 0)

    def gather_next():
        for j in range(per_step // SUBLANES):
            for i in range(SUBLANES):
                row_copy(nxt, c * (per_step // SUBLANES) + j, i, 1 - slot).start()

    @pl.when(live)
    def _():
        gather_next()
        xb = xb_ref[...]
        g = jnp.dot(xb, wg_ref[0], preferred_element_type=F32)
        u = jnp.dot(xb, wu_ref[0], preferred_element_type=F32)
        a = (g * jax.nn.sigmoid(g) * u).astype(BF16)
        acc_ref[...] += jnp.dot(a, wd_ref[0], preferred_element_type=F32)

    @pl.when(jnp.logical_not(live))
    def _():
        gather_next()

    @pl.when(c == n_c - 1)
    def _():
        o_ref[...] = acc_ref[...]

    @pl.when(jnp.logical_and(blk == n_blk - 1, c == n_c - 1))
    def _():
        wait_block(1 - slot)


def moe_experts(h, slot_tok, block_e, n_live, w_gu, w_d, tm=MOE_TM, tc=MOE_TC):
    n_slots = slot_tok.shape[0]
    d = h.shape[1]
    d_e = w_d.shape[1]
    n_c = d_e // tc
    assert tm % (n_c * SUBLANES) == 0, "each grid step gathers an equal share of the next block's row groups"
    assert tm % WAIT_RUN == 0 and WAIT_RUN % SUBLANES == 0
    grid_spec = pltpu.PrefetchScalarGridSpec(
        num_scalar_prefetch=3,
        grid=(n_slots // tm, n_c),
        in_specs=[pl.BlockSpec(memory_space=pl.ANY),
                  pl.BlockSpec((1, d, tc), lambda b, c, be, tok, nl: (be[b], 0, c)),
                  pl.BlockSpec((1, d, tc), lambda b, c, be, tok, nl: (be[b], 0, n_c + c)),
                  pl.BlockSpec((1, tc, d), lambda b, c, be, tok, nl: (be[b], c, 0))],
        out_specs=pl.BlockSpec((tm, d), lambda b, c, be, tok, nl: (b, 0)),
        scratch_shapes=[pltpu.VMEM((2, tm // SUBLANES, SUBLANES, d), F32), pltpu.VMEM((tm, d), BF16),
                        pltpu.VMEM((tm, d), F32), pltpu.SemaphoreType.DMA((2,))],
    )
    return pl.pallas_call(
        functools.partial(_expert_body, n_c=n_c),
        grid_spec=grid_spec,
        out_shape=jax.ShapeDtypeStruct((n_slots, d), F32),
        compiler_params=_cparams(("arbitrary", "arbitrary")),
        name="moe_experts",
    )(block_e, slot_tok, n_live, h, w_gu, w_gu, w_d)


def _combine_body(dest_ref, x_ref, r_ref, yb_ref, o_ref, buf_ref, sem, *, tb):
    step = pl.program_id(0)
    slot = step % 2

    def issue(st, s):
        def body(g, carry):
            for i in range(SUBLANES):
                for k in range(2):
                    src = yb_ref.at[pl.ds(dest_ref[2 * (st * tb + g * SUBLANES + i) + k], 1)]
                    pltpu.make_async_copy(src, buf_ref.at[s, k, g, pl.ds(i, 1)], sem.at[s]).start()
            return carry
        lax.fori_loop(0, tb // SUBLANES, body, 0)

    def wait_all(s):
        def body(g, carry):
            for j in range(WAIT_RUN // SUBLANES):
                for i in range(SUBLANES):
                    for k in range(2):
                        dst = buf_ref.at[s, k, g * (WAIT_RUN // SUBLANES) + j, pl.ds(i, 1)]
                        pltpu.make_async_copy(yb_ref.at[pl.ds(0, 1)], dst, sem.at[s]).wait()
            return carry
        lax.fori_loop(0, tb // WAIT_RUN, body, 0)

    @pl.when(step == 0)
    def _():
        issue(0, 0)

    @pl.when(step + 1 < pl.num_programs(0))
    def _():
        issue(step + 1, 1 - slot)

    wait_all(slot)
    route = r_ref[...]
    y0 = buf_ref[slot, 0].reshape(o_ref.shape)
    y1 = buf_ref[slot, 1].reshape(o_ref.shape)
    o_ref[...] = x_ref[...] + route[:, 2:3] * y0 + route[:, 3:4] * y1


def moe_combine(x, route, yb, dest, tb=256):
    t, d = x.shape
    assert tb % WAIT_RUN == 0
    grid_spec = pltpu.PrefetchScalarGridSpec(
        num_scalar_prefetch=1,
        grid=(t // tb,),
        in_specs=[pl.BlockSpec((tb, d), lambda i, dest: (i, 0)),
                  pl.BlockSpec((tb, HEAD_W), lambda i, dest: (i, 0)),
                  pl.BlockSpec(memory_space=pl.ANY)],
        out_specs=pl.BlockSpec((tb, d), lambda i, dest: (i, 0)),
        scratch_shapes=[pltpu.VMEM((2, 2, tb // SUBLANES, SUBLANES, d), F32), pltpu.SemaphoreType.DMA((2,))],
    )
    return pl.pallas_call(
        functools.partial(_combine_body, tb=tb),
        grid_spec=grid_spec,
        out_shape=jax.ShapeDtypeStruct((t, d), F32),
        compiler_params=_cparams(("arbitrary",)),
        name="moe_combine",
    )(dest, x, route, yb)


def moe_layer(hn, x, route, w_gu, w_d, tm=MOE_TM):
    t, d = x.shape
    n_assign = 2 * t
    e_flat = route[:, :2].astype(jnp.int32).reshape(n_assign)
    onehot = (e_flat[:, None] == jnp.arange(N_EXPERTS, dtype=jnp.int32)[None, :]).astype(jnp.int32)
    incl = jnp.cumsum(onehot, axis=0)
    rank = jnp.sum((incl - onehot) * onehot, axis=1)
    counts = incl[-1]
    padded = (counts + tm - 1) // tm * tm
    pad_end = jnp.cumsum(padded)
    pad_start = pad_end - padded
    dest = (pad_start[e_flat] + rank).astype(jnp.int32)
    n_blocks = -(-n_assign // tm) + N_EXPERTS
    block_start = jnp.arange(n_blocks, dtype=jnp.int32) * tm
    block_e = jnp.minimum(jnp.sum((pad_end[None, :] <= block_start[:, None]).astype(jnp.int32), axis=1),
                          N_EXPERTS - 1).astype(jnp.int32)
    slot_tok = jnp.zeros((n_blocks * tm,), jnp.int32).at[dest].set(jnp.arange(n_assign, dtype=jnp.int32) // 2,
                                                                  unique_indices=True)
    n_live = (pad_end[-1:] // tm).astype(jnp.int32)
    yb = moe_experts(hn, slot_tok, block_e, n_live, w_gu, w_d)
    return moe_combine(x, route, yb, dest)


def _rope_coefficients(positions):
    half = ROT_DIM // 2
    lane = jnp.arange(HEAD_W) % ATT_DIM
    inv_freq = ROPE_THETA ** (-(2.0 * (lane % half)).astype(F32) / ROT_DIM)
    freq = jnp.where(lane < ROT_DIM, inv_freq, 0.0)
    ang = positions.astype(F32).reshape(-1, 1) * freq[None, :]
    sin = jnp.sin(ang)
    return (jnp.cos(ang), jnp.where(lane < half, -sin, 0.0),
            jnp.where((lane >= half) & (lane < ROT_DIM), sin, 0.0))


def _split_w_in(w_in):
    att = 3 * GROUP_W
    conv = 3 * GROUP_W
    o_z = att + conv
    o_small = o_z + GROUP_W
    o_ga = o_small + 2 * GDN_HEADS
    d = w_in.shape[0]
    main = jnp.concatenate([w_in[:, o_ga:], w_in[:, :o_small]], axis=1).astype(BF16)
    small = jnp.zeros((d, HEAD_W), F32).at[:, :2 * GDN_HEADS].set(w_in[:, o_small:o_ga]).astype(BF16)
    return main, small


def kernel(x, positions, norm_mix_w, w_in, q_norm_w, k_norm_w, lam_vec, subln_w, conv_w, a_log, dt_bias,
           gdn_norm_w, w_branch_a, w_branch_b, w_out, norm_ffn_w, ffn_w_gate_up, ffn_w_down, router_w,
           router_b, moe_w_gate_up, moe_w_down):
    batch, seq, d = x.shape
    depth = w_in.shape[0]
    t = batch * seq
    xt = x.reshape(t, d).astype(F32)
    rope_c, rope_s1, rope_s2 = _rope_coefficients(positions)
    h = rmsnorm(xt, norm_mix_w[0].astype(F32))
    for layer in range(depth):
        lam_init = 0.8 - 0.6 * math.exp(-0.3 * layer)
        is_moe = layer % 2 == 1
        j = layer // 2
        w_main, w_small = _split_w_in(w_in[layer])
        proj, small = in_proj(h, w_main, w_small)
        qs, kk, vx = attn_prep(proj, rope_c, rope_s1, rope_s2, q_norm_w[layer], k_norm_w[layer])
        oa = flash_diff_attention(qs, kk, vx, q_norm_w[layer], k_norm_w[layer], lam_vec[layer], subln_w[layer],
                                  lam_init, batch, seq)
        ob = gated_delta_net(proj, small, conv_w[layer], a_log[layer], dt_bias[layer], gdn_norm_w[layer], batch, seq)
        router = (router_w[j], router_b[j]) if is_moe else None
        outs = mixer_out(oa, ob, proj, xt, w_branch_a[layer].astype(BF16), w_branch_b[layer].astype(BF16),
                         w_out[layer].astype(BF16), norm_ffn_w[layer], router)
        if is_moe:
            xt, hn, route = outs
            xt = moe_layer(hn, xt, route, moe_w_gate_up[j].astype(BF16), moe_w_down[j].astype(BF16))
            if layer + 1 < depth:
                h = rmsnorm(xt, norm_mix_w[layer + 1].astype(F32))
        else:
            xt, hn = outs
            nxt = norm_mix_w[layer + 1] if layer + 1 < depth else None
            res = dense_ffn(hn, xt, ffn_w_gate_up[j].astype(BF16), ffn_w_down[j].astype(BF16), nxt)
            if nxt is not None:
                xt, h = res
            else:
                xt = res[0]
    return xt.reshape(batch, seq, d)
```

```python
import functools
import math

import jax
import jax.numpy as jnp
from jax import lax
from jax.experimental import pallas as pl
from jax.experimental.pallas import tpu as pltpu

F32 = jnp.float32
BF16 = jnp.bfloat16

ATT_HEADS = 4
ATT_DIM = 64
ROT_DIM = ATT_DIM // 4
ROPE_THETA = 500000.0
GDN_HEADS = 4
GDN_DK = 128
CONV_K = 4
N_EXPERTS = 8
NORM_EPS = 1e-6

HEAD_W = 128
GROUP_W = 512
COL_GA, COL_GB, COL_AQ, COL_AK, COL_AV, COL_BQ, COL_BK, COL_BV, COL_BZ = 0, 2, 4, 5, 6, 7, 8, 9, 10
N_GROUPS = 11

VMEM_LIMIT = 56 * 1024 * 1024

GDN_CHUNK = 256
GDN_STREAMS = 2
ATT_BLOCK = 1024
ATT_CHUNK = 256
LOG2E = math.log2(math.e)
BOUND_SLACK = 1.01
MAX_SAFE_LOG2_BOUND = 56.0
MOE_TM = 896
MOE_TC = 512
WAIT_RUN = 16
SUBLANES = 8


def _cparams(sem):
    return pltpu.CompilerParams(dimension_semantics=sem, vmem_limit_bytes=VMEM_LIMIT)


def _bdot(a, b):
    return jnp.dot(a.astype(BF16), b.astype(BF16), preferred_element_type=F32)


def _bdot_nt(a, b):
    return lax.dot_general(a.astype(BF16), b.astype(BF16), (((1,), (1,)), ((), ())),
                           preferred_element_type=F32)


def _bdot_tn(a, b):
    return lax.dot_general(a.astype(BF16), b.astype(BF16), (((0,), (0,)), ((), ())),
                           preferred_element_type=F32)


def _hdot(a, b):
    return jnp.dot(a, b, preferred_element_type=F32, precision=lax.Precision.HIGHEST)


def _rms(x, w):
    return x * lax.rsqrt(jnp.mean(x * x, axis=-1, keepdims=True) + NORM_EPS) * w


def _rmsnorm_body(x_ref, w_ref, o_ref):
    o_ref[...] = _rms(x_ref[...], w_ref[...]).astype(o_ref.dtype)


def rmsnorm(x, w, out_dtype=BF16, tm=1024):
    t, d = x.shape
    return pl.pallas_call(
        _rmsnorm_body,
        grid=(t // tm,),
        in_specs=[pl.BlockSpec((tm, d), lambda i: (i, 0)), pl.BlockSpec((1, d), lambda i: (0, 0))],
        out_specs=pl.BlockSpec((tm, d), lambda i: (i, 0)),
        out_shape=jax.ShapeDtypeStruct((t, d), out_dtype),
        compiler_params=_cparams(("parallel",)),
        name="rmsnorm",
    )(x, w.reshape(1, d))


def _in_proj_body(h_ref, w_ref, ws_ref, o_ref, os_ref):
    h = h_ref[...]
    o_ref[...] = jnp.dot(h, w_ref[...], preferred_element_type=F32)

    @pl.when(pl.program_id(1) == 0)
    def _():
        os_ref[...] = jnp.dot(h, ws_ref[...], preferred_element_type=F32)


def in_proj(h, w_main, w_small, tm=1024, n_col_steps=2):
    t, d = h.shape
    n = w_main.shape[1]
    tn = n // n_col_steps
    return pl.pallas_call(
        _in_proj_body,
        grid=(t // tm, n_col_steps),
        in_specs=[pl.BlockSpec((tm, d), lambda i, j: (i, 0)),
                  pl.BlockSpec((d, tn), lambda i, j: (0, j)),
                  pl.BlockSpec((d, HEAD_W), lambda i, j: (0, 0))],
        out_specs=[pl.BlockSpec((tm, tn), lambda i, j: (i, j)),
                   pl.BlockSpec((tm, HEAD_W), lambda i, j: (i, 0))],
        out_shape=[jax.ShapeDtypeStruct((t, n), F32), jax.ShapeDtypeStruct((t, HEAD_W), F32)],
        compiler_params=_cparams(("parallel", "arbitrary")),
        name="in_proj",
    )(h, w_main, w_small)


def _group_mean_sq(x, m_ref):
    ss = x * x
    hi = ss.astype(BF16)
    lo = (ss - hi.astype(F32)).astype(BF16)
    m = m_ref[...]
    gs = jnp.dot(hi, m, preferred_element_type=F32) + jnp.dot(lo, m, preferred_element_type=F32)
    return gs * (1.0 / ATT_DIM)


def _attn_prep_body(q_ref, k_ref, v_ref, c_ref, s1_ref, s2_ref, qw_ref, kw_ref, m_ref,
                    qs_ref, ko_ref, vx_ref):
    width = q_ref.shape[1]
    reps = width // HEAD_W
    c = jnp.concatenate([c_ref[...]] * reps, axis=1)
    s1 = jnp.concatenate([s1_ref[...]] * reps, axis=1)
    s2 = jnp.concatenate([s2_ref[...]] * reps, axis=1)

    def norm_rope(x, w):
        y = x * lax.rsqrt(_group_mean_sq(x, m_ref) + NORM_EPS) * w
        half = ROT_DIM // 2
        return y * c + pltpu.roll(y, width - half, 1) * s1 + pltpu.roll(y, half, 1) * s2

    q = norm_rope(q_ref[...], qw_ref[...]) * (ATT_DIM ** -0.5 * LOG2E)
    lane = lax.broadcasted_iota(jnp.int32, q.shape, 1)
    first_map = (lane % HEAD_W) < ATT_DIM
    qs_ref[0] = jnp.where(first_map, q, 0.0).astype(BF16)
    qs_ref[1] = jnp.where(first_map, 0.0, q).astype(BF16)
    ko_ref[...] = norm_rope(k_ref[...], kw_ref[...]).astype(BF16)
    v = v_ref[...].astype(BF16)
    ones = jnp.ones((v.shape[0], HEAD_W), BF16)
    parts = []
    for h in range(reps):
        parts += [v[:, h * HEAD_W:(h + 1) * HEAD_W], ones]
    vx_ref[...] = jnp.concatenate(parts, axis=1)


def attn_prep(proj, rope_c, rope_s1, rope_s2, q_norm_w, k_norm_w, tm=512):
    t = proj.shape[0]
    w = GROUP_W
    grp = jnp.arange(w) // ATT_DIM
    ones_bd = (grp[:, None] == grp[None, :]).astype(BF16)
    qw = jnp.tile(q_norm_w.astype(F32), w // ATT_DIM).reshape(1, w)
    kw = jnp.tile(k_norm_w.astype(F32), w // ATT_DIM).reshape(1, w)
    col = lambda g: pl.BlockSpec((tm, w), lambda i, g=g: (i, g))
    tab = pl.BlockSpec((tm, HEAD_W), lambda i: (i, 0))
    const = lambda shape: pl.BlockSpec(shape, lambda i: (0, 0))
    return pl.pallas_call(
        _attn_prep_body,
        grid=(t // tm,),
        in_specs=[col(COL_AQ), col(COL_AK), col(COL_AV), tab, tab, tab,
                  const((1, w)), const((1, w)), const((w, w))],
        out_specs=[pl.BlockSpec((2, tm, w), lambda i: (0, i, 0)), pl.BlockSpec((tm, w), lambda i: (i, 0)),
                   pl.BlockSpec((tm, 2 * w), lambda i: (i, 0))],
        out_shape=[jax.ShapeDtypeStruct((2, t, w), BF16), jax.ShapeDtypeStruct((t, w), BF16),
                   jax.ShapeDtypeStruct((t, 2 * w), BF16)],
        compiler_params=_cparams(("parallel",)),
        name="attn_prep",
    )(proj, proj, proj, rope_c, rope_s1, rope_s2, qw, kw, ones_bd)


def _flash_body(qi_ref, ki_ref, fast_ref, bound_ref, q_ref, k_ref, vx_ref, lam_ref, sw_ref, o_ref,
                m_ref, acc_ref, *, lam_init, chunk):
    p_idx = pl.program_id(1)
    qi = qi_ref[p_idx]
    ki = ki_ref[p_idx]
    tq = q_ref.shape[1]
    tk = k_ref.shape[0]
    fast = fast_ref[0] == 1
    diag = ki == qi

    @pl.when(ki == 0)
    def _():
        m_ref[...] = jnp.full(m_ref.shape, -jnp.inf, F32)
        acc_ref[...] = jnp.zeros(acc_ref.shape, F32)

    def keep_mask(c0, width):
        row = lax.broadcasted_iota(jnp.int32, (2 * tq, width), 0)
        col = lax.broadcasted_iota(jnp.int32, (2 * tq, width), 1) + c0
        return col <= jnp.where(row >= tq, row - tq, row)

    def head_operands(h):
        qk_lanes = pl.ds(pl.multiple_of(h * HEAD_W, HEAD_W), HEAD_W)
        v_lanes = pl.ds(pl.multiple_of(h * 2 * HEAD_W, 2 * HEAD_W), 2 * HEAD_W)
        return q_ref[:, :, qk_lanes].reshape(2 * tq, HEAD_W), qk_lanes, v_lanes

    def update_bounded(h, masked):
        q2, qk_lanes, v_lanes = head_operands(h)
        bound = bound_ref[0]
        for c0 in range(0, tk, chunk):
            k_c = k_ref[c0:c0 + chunk, qk_lanes]
            v_c = vx_ref[c0:c0 + chunk, v_lanes]
            if not masked:
                s = lax.dot_general(q2, k_c, (((1,), (1,)), ((), ())), preferred_element_type=F32)
                acc_ref[h] += jnp.dot(jnp.exp2(s - bound).astype(BF16), v_c, preferred_element_type=F32)
                continue
            rows = tq - c0
            q_low = q_ref[:, c0:, qk_lanes].reshape(2 * rows, HEAD_W)
            s = lax.dot_general(q_low, k_c, (((1,), (1,)), ((), ())), preferred_element_type=F32)
            row = lax.broadcasted_iota(jnp.int32, (2 * rows, chunk), 0)
            col = lax.broadcasted_iota(jnp.int32, (2 * rows, chunk), 1)
            keep = col <= jnp.where(row >= rows, row - rows, row)
            p = jnp.where(keep, jnp.exp2(s - bound), 0.0).astype(BF16)
            part = jnp.dot(p, v_c, preferred_element_type=F32)
            acc_ref[h, c0:tq] += part[:rows]
            acc_ref[h, tq + c0:] += part[rows:]

    def update_running_max(h, masked):
        q2, qk_lanes, v_lanes = head_operands(h)
        s = lax.dot_general(q2, k_ref[:, qk_lanes], (((1,), (1,)), ((), ())), preferred_element_type=F32)
        if masked:
            s = jnp.where(keep_mask(0, tk), s, -jnp.inf)
        m_prev = m_ref[h]
        m_new = jnp.maximum(m_prev, jnp.max(s, axis=-1, keepdims=True))
        p = jnp.exp2(s - m_new)
        acc_ref[h] = (jnp.exp2(m_prev - m_new) * acc_ref[h]
                      + jnp.dot(p.astype(BF16), vx_ref[:, v_lanes], preferred_element_type=F32))
        m_ref[h] = m_new

    def all_heads(update, masked):
        def body(h, carry):
            update(h, masked)
            return carry
        lax.fori_loop(0, ATT_HEADS, body, 0)

    for use_fast, update in ((True, update_bounded), (False, update_running_max)):
        path = fast if use_fast else jnp.logical_not(fast)

        @pl.when(jnp.logical_and(path, jnp.logical_not(diag)))
        def _(update=update):
            all_heads(update, False)

        @pl.when(jnp.logical_and(path, diag))
        def _(update=update):
            all_heads(update, True)

    @pl.when(diag)
    def _():
        lv = lam_ref[...]
        lam = (jnp.exp(jnp.sum(lv[0:1] * lv[1:2], axis=-1, keepdims=True))
               - jnp.exp(jnp.sum(lv[2:3] * lv[3:4], axis=-1, keepdims=True)) + lam_init)
        for h in range(ATT_HEADS):
            o = (acc_ref[h, 0:tq, 0:HEAD_W] / acc_ref[h, 0:tq, HEAD_W:]
                 - lam * (acc_ref[h, tq:, 0:HEAD_W] / acc_ref[h, tq:, HEAD_W:]))
            o_ref[:, h * HEAD_W:(h + 1) * HEAD_W] = (_rms(o, sw_ref[...]) * (1.0 - lam_init)).astype(o_ref.dtype)


def flash_diff_attention(qs, k, vx, q_norm_w, k_norm_w, lam_vec, subln_w, lam_init, batch, seq,
                         blk=ATT_BLOCK, chunk=ATT_CHUNK):
    t = k.shape[0]
    nq = seq // blk
    w = ATT_HEADS * HEAD_W
    pairs = [(i, j) for i in range(nq) for j in range(i + 1)]
    qi_tab = jnp.asarray([p[0] for p in pairs], jnp.int32)
    ki_tab = jnp.asarray([p[1] for p in pairs], jnp.int32)
    bound = (ATT_DIM ** 0.5) * jnp.max(jnp.abs(q_norm_w.astype(F32))) * jnp.max(jnp.abs(k_norm_w.astype(F32)))
    bound = bound * (LOG2E * BOUND_SLACK)
    fast = (bound <= MAX_SAFE_LOG2_BOUND).astype(jnp.int32).reshape(1)
    idx = lambda f: (lambda b, p, qi, ki, fast: f(b, qi[p], ki[p]))
    grid_spec = pltpu.PrefetchScalarGridSpec(
        num_scalar_prefetch=3,
        grid=(batch, len(pairs)),
        in_specs=[pl.BlockSpec(memory_space=pltpu.SMEM),
                  pl.BlockSpec((2, blk, w), idx(lambda b, qi, ki: (0, b * nq + qi, 0))),
                  pl.BlockSpec((blk, w), idx(lambda b, qi, ki: (b * nq + ki, 0))),
                  pl.BlockSpec((blk, 2 * w), idx(lambda b, qi, ki: (b * nq + ki, 0))),
                  pl.BlockSpec((4, ATT_DIM), idx(lambda b, qi, ki: (0, 0))),
                  pl.BlockSpec((1, HEAD_W), idx(lambda b, qi, ki: (0, 0)))],
        out_specs=pl.BlockSpec((blk, w), idx(lambda b, qi, ki: (b * nq + qi, 0))),
        scratch_shapes=[pltpu.VMEM((ATT_HEADS, 2 * blk, 1), F32),
                        pltpu.VMEM((ATT_HEADS, 2 * blk, 2 * HEAD_W), F32)],
    )
    return pl.pallas_call(
        functools.partial(_flash_body, lam_init=lam_init, chunk=chunk),
        grid_spec=grid_spec,
        out_shape=jax.ShapeDtypeStruct((t, w), BF16),
        compiler_params=_cparams(("parallel", "arbitrary")),
        name="flash_diff_attention",
    )(qi_tab, ki_tab, fast, bound.reshape(1), qs, k, vx, lam_vec.astype(F32),
      subln_w.astype(F32).reshape(1, HEAD_W))


def _softplus(x):
    return jnp.maximum(x, 0.0) + jnp.log1p(jnp.exp(-jnp.abs(x)))


def _conv_silu(x, halo, w):
    head = x[0:8]
    row8 = lax.broadcasted_iota(jnp.int32, head.shape, 0)
    y = x * w[CONV_K - 1:CONV_K]
    y_head = head * w[CONV_K - 1:CONV_K]
    for j in range(1, CONV_K):
        wj = w[CONV_K - 1 - j:CONV_K - j]
        y = y + pltpu.roll(x, j, 0) * wj
        shifted = jnp.where(row8 < j, pltpu.roll(halo, j, 0), pltpu.roll(head, j, 0))
        y_head = y_head + shifted * wj
    y = jnp.concatenate([y_head, y[8:]], axis=0)
    return y * jax.nn.sigmoid(y)


def _l2n(x):
    return x * lax.rsqrt(jnp.sum(x * x, axis=-1, keepdims=True) + NORM_EPS)


def _gdn_body(q_ref, k_ref, v_ref, hq_ref, hk_ref, hv_ref, sm_ref, cq_ref, ck_ref, cv_ref, al_ref, dt_ref,
              z_ref, nw_ref, o_ref, state_ref):
    n_streams, c = q_ref.shape[0], q_ref.shape[1]
    heads = range(GDN_HEADS)
    items = [(s, h) for s in range(n_streams) for h in heads]
    seq_start = pl.program_id(1) == 0

    @pl.when(seq_start)
    def _():
        state_ref[...] = jnp.zeros(state_ref.shape, F32)

    def conv(x_ref, halo_ref, w_ref, s):
        return _conv_silu(x_ref[s], jnp.where(seq_start, 0.0, halo_ref[s]), w_ref[...])

    row = lax.broadcasted_iota(jnp.int32, (c, c), 0)
    col = lax.broadcasted_iota(jnp.int32, (c, c), 1)
    incl = row >= col
    strict = row > col
    xor_idx = row ^ col
    lower_ones = jnp.where(incl, 1.0, 0.0).astype(F32)
    upper_ones = jnp.where(row <= col, 1.0, 0.0).astype(F32)
    sl = [slice(h * HEAD_W, (h + 1) * HEAD_W) for h in heads]

    q, k, v_beta, beta, gcol, grow = {}, {}, {}, {}, {}, {}
    for s in range(n_streams):
        q_all = conv(q_ref, hq_ref, cq_ref, s)
        k_all = conv(k_ref, hk_ref, ck_ref, s)
        v_all = conv(v_ref, hv_ref, cv_ref, s)
        raw = sm_ref[s]
        lane = lax.broadcasted_iota(jnp.int32, raw.shape, 1)
        sm = jnp.where(lane < GDN_HEADS, jax.nn.sigmoid(raw),
                       jnp.where(lane < 2 * GDN_HEADS, -jnp.exp(al_ref[...]) * _softplus(raw + dt_ref[...]), 0.0))
        gc_col = _hdot(lower_ones, sm)
        gc_row = _hdot(sm.T[0:8], upper_ones)
        for h in heads:
            q[s, h] = _l2n(q_all[:, sl[h]]) * (GDN_DK ** -0.5)
            k[s, h] = _l2n(k_all[:, sl[h]])
            beta[s, h] = sm[:, h:h + 1]
            v_beta[s, h] = v_all[:, sl[h]] * beta[s, h]
            gcol[s, h] = gc_col[:, GDN_HEADS + h:GDN_HEADS + h + 1]
            grow[s, h] = gc_row[GDN_HEADS + h:GDN_HEADS + h + 1, :]

    decay = {i: jnp.where(incl, jnp.exp(jnp.minimum(gcol[i] - grow[i], 0.0)), 0.0) for i in items}
    kb = {i: k[i] * beta[i] for i in items}
    l_mat = {i: jnp.where(strict, _bdot_nt(kb[i], k[i]) * decay[i], 0.0) for i in items}
    a_qk = {i: _bdot_nt(q[i], k[i]) * decay[i] for i in items}
    eg = {i: jnp.exp(gcol[i]) for i in items}
    rhs = {i: jnp.concatenate([v_beta[i], kb[i] * eg[i]], axis=1) for i in items}
    eye = jnp.where(row == col, 1.0, 0.0).astype(F32)
    x = {i: eye for i in items}
    b = 1
    while b < c:
        level = (xor_idx >= b) & (xor_idx < 2 * b)
        ex = {i: _bdot(jnp.where(level, l_mat[i], 0.0), x[i]) for i in items}
        x = {i: x[i] - _bdot(x[i], ex[i]) for i in items}
        b *= 2
    sol = {i: _bdot(x[i], rhs[i]) for i in items}
    state = {(s, h): state_ref[s * GDN_HEADS + h] for s, h in items}
    v_new = {i: sol[i][:, :HEAD_W] - _bdot(sol[i][:, HEAD_W:], state[i]) for i in items}
    o = {i: _bdot(q[i] * eg[i], state[i]) + _bdot(a_qk[i], v_new[i]) for i in items}
    nw = nw_ref[...]
    for s, h in items:
        i = (s, h)
        g_last = gcol[i][c - 1:c, :]
        k_dec = k[i] * jnp.exp(g_last - gcol[i])
        state_ref[s * GDN_HEADS + h] = state[i] * jnp.exp(g_last) + _bdot_tn(k_dec, v_new[i])
        z = z_ref[s, :, sl[h]]
        o_ref[s, :, sl[h]] = (_rms(o[i], nw) * (z * jax.nn.sigmoid(z))).astype(o_ref.dtype)


def gated_delta_net(proj, small, conv_w, a_log, dt_bias, gdn_norm_w, batch, seq, chunk=GDN_CHUNK):
    t = proj.shape[0]
    nc = seq // chunk
    w = GROUP_W
    n_streams = GDN_STREAMS if batch % GDN_STREAMS == 0 else 1
    groups = batch // n_streams
    proj3 = proj.reshape(n_streams, t // n_streams, proj.shape[1])
    small3 = small.reshape(n_streams, t // n_streams, HEAD_W)
    cw = conv_w.astype(F32)
    pad = lambda vec: jnp.zeros((1, HEAD_W), F32).at[0, GDN_HEADS:2 * GDN_HEADS].set(vec.astype(F32))
    col = lambda g: pl.BlockSpec((n_streams, chunk, w), lambda b, c, g=g: (0, b * nc + c, g))
    halo = lambda g: pl.BlockSpec((n_streams, 8, w),
                                  lambda b, c, g=g: (0, jnp.maximum((b * nc + c) * (chunk // 8) - 1, 0), g))
    const = lambda shape: pl.BlockSpec(shape, lambda b, c: (0, 0))
    out = pl.pallas_call(
        _gdn_body,
        grid=(groups, nc),
        in_specs=[col(COL_BQ), col(COL_BK), col(COL_BV), halo(COL_BQ), halo(COL_BK), halo(COL_BV),
                  pl.BlockSpec((n_streams, chunk, HEAD_W), lambda b, c: (0, b * nc + c, 0)),
                  const((CONV_K, w)), const((CONV_K, w)), const((CONV_K, w)),
                  const((1, HEAD_W)), const((1, HEAD_W)), col(COL_BZ), const((1, HEAD_W))],
        out_specs=pl.BlockSpec((n_streams, chunk, w), lambda b, c: (0, b * nc + c, 0)),
        out_shape=jax.ShapeDtypeStruct((n_streams, t // n_streams, w), BF16),
        scratch_shapes=[pltpu.VMEM((n_streams * GDN_HEADS, GDN_DK, HEAD_W), F32)],
        compiler_params=_cparams(("parallel", "arbitrary")),
        name="gated_delta_net",
    )(proj3, proj3, proj3, proj3, proj3, proj3, small3, cw[:, 0:w], cw[:, w:2 * w], cw[:, 2 * w:3 * w],
      pad(a_log), pad(dt_bias), proj3, gdn_norm_w.astype(F32).reshape(1, HEAD_W))
    return out.reshape(t, w)


def _top2_route(logits):
    lane = lax.broadcasted_iota(jnp.int32, logits.shape, 1).astype(F32)
    big = float(logits.shape[1])
    m1 = jnp.max(logits, axis=-1, keepdims=True)
    i1 = jnp.min(jnp.where(logits == m1, lane, big), axis=-1, keepdims=True)
    rest = jnp.where(lane == i1, -jnp.inf, logits)
    m2 = jnp.max(rest, axis=-1, keepdims=True)
    i2 = jnp.min(jnp.where(rest == m2, lane, big), axis=-1, keepdims=True)
    e = jnp.exp(m2 - m1)
    g1 = 1.0 / (1.0 + e)
    g2 = e / (1.0 + e)
    return jnp.where(lane == 0, i1, jnp.where(lane == 1, i2, jnp.where(lane == 2, g1, jnp.where(lane == 3, g2, 0.0))))


def _mixer_out_body(*refs, with_router):
    if with_router:
        (oa_ref, ob_ref, ga_ref, gb_ref, x_ref, wa_ref, wb_ref, wo_ref, nw_ref, rwh_ref, rwl_ref, rb_ref,
         xo_ref, ho_ref, ro_ref) = refs
    else:
        (oa_ref, ob_ref, ga_ref, gb_ref, x_ref, wa_ref, wb_ref, wo_ref, nw_ref, xo_ref, ho_ref) = refs
    ya = jnp.dot(oa_ref[...], wa_ref[...], preferred_element_type=F32)
    yb = jnp.dot(ob_ref[...], wb_ref[...], preferred_element_type=F32)
    merged = jax.nn.sigmoid(ga_ref[...]) * ya + jax.nn.sigmoid(gb_ref[...]) * yb
    x_new = x_ref[...] + jnp.dot(merged.astype(BF16), wo_ref[...], preferred_element_type=F32)
    xo_ref[...] = x_new
    hn = _rms(x_new, nw_ref[...])
    ho_ref[...] = hn.astype(ho_ref.dtype)
    if with_router:
        hn_hi = hn.astype(BF16)
        hn_lo = (hn - hn_hi.astype(F32)).astype(BF16)
        rw_hi = rwh_ref[...]
        logits = (jnp.dot(hn_hi, rw_hi, preferred_element_type=F32)
                  + jnp.dot(hn_lo, rw_hi, preferred_element_type=F32)
                  + jnp.dot(hn_hi, rwl_ref[...], preferred_element_type=F32))
        ro_ref[...] = _top2_route(logits + rb_ref[...])


def mixer_out(oa, ob, proj, x, w_a, w_b, w_o, norm_w, router=None, tm=512):
    t, d = x.shape
    w = GROUP_W
    row = lambda width, g=0: pl.BlockSpec((tm, width), lambda i, g=g: (i, g))
    const = lambda shape: pl.BlockSpec(shape, lambda i: (0, 0))
    in_specs = [row(w), row(w), row(d, COL_GA * w // d), row(d, COL_GB * w // d), row(d),
                const((w, d)), const((w, d)), const((d, d)), const((1, d))]
    args = [oa, ob, proj, proj, x, w_a, w_b, w_o, norm_w.astype(F32).reshape(1, d)]
    out_specs = [row(d), row(d)]
    out_shape = [jax.ShapeDtypeStruct((t, d), F32),
                 jax.ShapeDtypeStruct((t, d), F32 if router is not None else BF16)]
    if router is not None:
        rw, rb = router
        rw_pad = jnp.zeros((d, HEAD_W), F32).at[:, :N_EXPERTS].set(rw.astype(F32))
        rw_hi = rw_pad.astype(BF16)
        rw_lo = (rw_pad - rw_hi.astype(F32)).astype(BF16)
        rb_pad = jnp.full((1, HEAD_W), -jnp.inf, F32).at[0, :N_EXPERTS].set(rb.astype(F32))
        in_specs += [const((d, HEAD_W)), const((d, HEAD_W)), const((1, HEAD_W))]
        args += [rw_hi, rw_lo, rb_pad]
        out_specs.append(row(HEAD_W))
        out_shape.append(jax.ShapeDtypeStruct((t, HEAD_W), F32))
    return pl.pallas_call(
        functools.partial(_mixer_out_body, with_router=router is not None),
        grid=(t // tm,),
        in_specs=in_specs,
        out_specs=out_specs,
        out_shape=out_shape,
        compiler_params=_cparams(("parallel",)),
        name="mixer_out",
    )(*args)


def _ffn_body(*refs, with_norm, tc):
    if with_norm:
        h_ref, x_ref, wgu_ref, wd_ref, nw_ref, xo_ref, ho_ref, acc_ref = refs
    else:
        h_ref, x_ref, wgu_ref, wd_ref, xo_ref, acc_ref = refs
    h = h_ref[...]
    d_ff = wd_ref.shape[0]
    for ci, c0 in enumerate(range(0, d_ff, tc)):
        g = jnp.dot(h, wgu_ref[:, c0:c0 + tc], preferred_element_type=F32)
        u = jnp.dot(h, wgu_ref[:, d_ff + c0:d_ff + c0 + tc], preferred_element_type=F32)
        a = (g * jax.nn.sigmoid(g) * u).astype(BF16)
        part = jnp.dot(a, wd_ref[c0:c0 + tc, :], preferred_element_type=F32)
        if ci == 0:
            acc_ref[...] = x_ref[...] + part
        else:
            acc_ref[...] += part
    x_new = acc_ref[...]
    xo_ref[...] = x_new
    if with_norm:
        ho_ref[...] = _rms(x_new, nw_ref[...]).astype(ho_ref.dtype)


def dense_ffn(h, x, w_gu, w_d, next_norm_w=None, tm=512, tc=256):
    t, d = x.shape
    d_ff = w_d.shape[0]
    row = pl.BlockSpec((tm, d), lambda i: (i, 0))
    const = lambda shape: pl.BlockSpec(shape, lambda i: (0, 0), pipeline_mode=pl.Buffered(1))
    in_specs = [row, row, const((d, 2 * d_ff)), const((d_ff, d))]
    args = [h, x, w_gu, w_d]
    out_specs = [row]
    out_shape = [jax.ShapeDtypeStruct((t, d), F32)]
    if next_norm_w is not None:
        in_specs.append(pl.BlockSpec((1, d), lambda i: (0, 0)))
        args.append(next_norm_w.astype(F32).reshape(1, d))
        out_specs.append(row)
        out_shape.append(jax.ShapeDtypeStruct((t, d), BF16))
    return pl.pallas_call(
        functools.partial(_ffn_body, with_norm=next_norm_w is not None, tc=tc),
        grid=(t // tm,),
        in_specs=in_specs,
        out_specs=out_specs,
        out_shape=out_shape,
        scratch_shapes=[pltpu.VMEM((tm, d), F32)],
        compiler_params=_cparams(("parallel",)),
        name="dense_ffn",
    )(*args)


def _expert_body(be_ref, tok_ref, nl_ref, h_ref, wg_ref, wu_ref, wd_ref, o_ref, xbuf_ref, xb_ref, acc_ref, sem,
                 *, n_c):
    del be_ref
    blk = pl.program_id(0)
    c = pl.program_id(1)
    n_blk = pl.num_programs(0)
    tm = xb_ref.shape[0]
    slot = blk % 2
    live = blk < nl_ref[0]
    per_step = tm // n_c

    def row_copy(b, grp, i, s):
        tok = tok_ref[b * tm + grp * SUBLANES + i]
        return pltpu.make_async_copy(h_ref.at[pl.ds(tok, 1)], xbuf_ref.at[s, grp, pl.ds(i, 1)], sem.at[s])

    def wait_block(s):
        def body(g, carry):
            for j in range(WAIT_RUN // SUBLANES):
                for i in range(SUBLANES):
                    dst = xbuf_ref.at[s, g * (WAIT_RUN // SUBLANES) + j, pl.ds(i, 1)]
                    pltpu.make_async_copy(h_ref.at[pl.ds(0, 1)], dst, sem.at[s]).wait()
            return carry
        lax.fori_loop(0, tm // WAIT_RUN, body, 0)

    @pl.when(jnp.logical_and(blk == 0, c == 0))
    def _():
        def body(grp, carry):
            for i in range(SUBLANES):
                row_copy(0, grp, i, 0).start()
            return carry
        lax.fori_loop(0, tm // SUBLANES, body, 0)

    @pl.when(c == 0)
    def _():
        wait_block(slot)
        xb_ref[...] = xbuf_ref[slot].reshape(tm, xb_ref.shape[1]).astype(BF16)
        acc_ref[...] = jnp.zeros(acc_ref.shape, F32)

    nxt = jnp.where(blk + 1 < n_blk, blk + 1, 0)

    def gather_next():
        for j in range(per_step // SUBLANES):
            for i in range(SUBLANES):
                row_copy(nxt, c * (per_step // SUBLANES) + j, i, 1 - slot).start()

    @pl.when(live)
    def _():
        gather_next()
        xb = xb_ref[...]
        g = jnp.dot(xb, wg_ref[0], preferred_element_type=F32)
        u = jnp.dot(xb, wu_ref[0], preferred_element_type=F32)
        a = (g * jax.nn.sigmoid(g) * u).astype(BF16)
        acc_ref[...] += jnp.dot(a, wd_ref[0], preferred_element_type=F32)

    @pl.when(jnp.logical_not(live))
    def _():
        gather_next()

    @pl.when(c == n_c - 1)
    def _():
        o_ref[...] = acc_ref[...]

    @pl.when(jnp.logical_and(blk == n_blk - 1, c == n_c - 1))
    def _():
        wait_block(1 - slot)


def moe_experts(h, slot_tok, block_e, n_live, w_gu, w_d, tm=MOE_TM, tc=MOE_TC):
    n_slots = slot_tok.shape[0]
    d = h.shape[1]
    d_e = w_d.shape[1]
    n_c = d_e // tc
    assert tm % (n_c * SUBLANES) == 0, "each grid step gathers an equal share of the next block's row groups"
    assert tm % WAIT_RUN == 0 and WAIT_RUN % SUBLANES == 0
    grid_spec = pltpu.PrefetchScalarGridSpec(
        num_scalar_prefetch=3,
        grid=(n_slots // tm, n_c),
        in_specs=[pl.BlockSpec(memory_space=pl.ANY),
                  pl.BlockSpec((1, d, tc), lambda b, c, be, tok, nl: (be[b], 0, c)),
                  pl.BlockSpec((1, d, tc), lambda b, c, be, tok, nl: (be[b], 0, n_c + c)),
                  pl.BlockSpec((1, tc, d), lambda b, c, be, tok, nl: (be[b], c, 0))],
        out_specs=pl.BlockSpec((tm, d), lambda b, c, be, tok, nl: (b, 0)),
        scratch_shapes=[pltpu.VMEM((2, tm // SUBLANES, SUBLANES, d), F32), pltpu.VMEM((tm, d), BF16),
                        pltpu.VMEM((tm, d), F32), pltpu.SemaphoreType.DMA((2,))],
    )
    return pl.pallas_call(
        functools.partial(_expert_body, n_c=n_c),
        grid_spec=grid_spec,
        out_shape=jax.ShapeDtypeStruct((n_slots, d), F32),
        compiler_params=_cparams(("arbitrary", "arbitrary")),
        name="moe_experts",
    )(block_e, slot_tok, n_live, h, w_gu, w_gu, w_d)


def _combine_body(dest_ref, x_ref, r_ref, yb_ref, o_ref, buf_ref, sem, *, tb):
    step = pl.program_id(0)
    slot = step % 2

    def issue(st, s):
        def body(g, carry):
            for i in range(SUBLANES):
                for k in range(2):
                    src = yb_ref.at[pl.ds(dest_ref[2 * (st * tb + g * SUBLANES + i) + k], 1)]
                    pltpu.make_async_copy(src, buf_ref.at[s, k, g, pl.ds(i, 1)], sem.at[s]).start()
            return carry
        lax.fori_loop(0, tb // SUBLANES, body, 0)

    def wait_all(s):
        def body(g, carry):
            for j in range(WAIT_RUN // SUBLANES):
                for i in range(SUBLANES):
                    for k in range(2):
                        dst = buf_ref.at[s, k, g * (WAIT_RUN // SUBLANES) + j, pl.ds(i, 1)]
                        pltpu.make_async_copy(yb_ref.at[pl.ds(0, 1)], dst, sem.at[s]).wait()
            return carry
        lax.fori_loop(0, tb // WAIT_RUN, body, 0)

    @pl.when(step == 0)
    def _():
        issue(0, 0)

    @pl.when(step + 1 < pl.num_programs(0))
    def _():
        issue(step + 1, 1 - slot)

    wait_all(slot)
    route = r_ref[...]
    y0 = buf_ref[slot, 0].reshape(o_ref.shape)
    y1 = buf_ref[slot, 1].reshape(o_ref.shape)
    o_ref[...] = x_ref[...] + route[:, 2:3] * y0 + route[:, 3:4] * y1


def moe_combine(x, route, yb, dest, tb=256):
    t, d = x.shape
    assert tb % WAIT_RUN == 0
    grid_spec = pltpu.PrefetchScalarGridSpec(
        num_scalar_prefetch=1,
        grid=(t // tb,),
        in_specs=[pl.BlockSpec((tb, d), lambda i, dest: (i, 0)),
                  pl.BlockSpec((tb, HEAD_W), lambda i, dest: (i, 0)),
                  pl.BlockSpec(memory_space=pl.ANY)],
        out_specs=pl.BlockSpec((tb, d), lambda i, dest: (i, 0)),
        scratch_shapes=[pltpu.VMEM((2, 2, tb // SUBLANES, SUBLANES, d), F32), pltpu.SemaphoreType.DMA((2,))],
    )
    return pl.pallas_call(
        functools.partial(_combine_body, tb=tb),
        grid_spec=grid_spec,
        out_shape=jax.ShapeDtypeStruct((t, d), F32),
        compiler_params=_cparams(("arbitrary",)),
        name="moe_combine",
    )(dest, x, route, yb)


def moe_layer(hn, x, route, w_gu, w_d, tm=MOE_TM):
    t, d = x.shape
    n_assign = 2 * t
    e_flat = route[:, :2].astype(jnp.int32).reshape(n_assign)
    onehot = (e_flat[:, None] == jnp.arange(N_EXPERTS, dtype=jnp.int32)[None, :]).astype(jnp.int32)
    incl = jnp.cumsum(onehot, axis=0)
    rank = jnp.sum((incl - onehot) * onehot, axis=1)
    counts = incl[-1]
    padded = (counts + tm - 1) // tm * tm
    pad_end = jnp.cumsum(padded)
    pad_start = pad_end - padded
    dest = (pad_start[e_flat] + rank).astype(jnp.int32)
    n_blocks = -(-n_assign // tm) + N_EXPERTS
    block_start = jnp.arange(n_blocks, dtype=jnp.int32) * tm
    block_e = jnp.minimum(jnp.sum((pad_end[None, :] <= block_start[:, None]).astype(jnp.int32), axis=1),
                          N_EXPERTS - 1).astype(jnp.int32)
    slot_tok = jnp.zeros((n_blocks * tm,), jnp.int32).at[dest].set(jnp.arange(n_assign, dtype=jnp.int32) // 2,
                                                                  unique_indices=True)
    n_live = (pad_end[-1:] // tm).astype(jnp.int32)
    yb = moe_experts(hn, slot_tok, block_e, n_live, w_gu, w_d)
    return moe_combine(x, route, yb, dest)


def _rope_coefficients(positions):
    half = ROT_DIM // 2
    lane = jnp.arange(HEAD_W) % ATT_DIM
    inv_freq = ROPE_THETA ** (-(2.0 * (lane % half)).astype(F32) / ROT_DIM)
    freq = jnp.where(lane < ROT_DIM, inv_freq, 0.0)
    ang = positions.astype(F32).reshape(-1, 1) * freq[None, :]
    sin = jnp.sin(ang)
    return (jnp.cos(ang), jnp.where(lane < half, -sin, 0.0),
            jnp.where((lane >= half) & (lane < ROT_DIM), sin, 0.0))


def _split_w_in(w_in):
    att = 3 * GROUP_W
    conv = 3 * GROUP_W
    o_z = att + conv
    o_small = o_z + GROUP_W
    o_ga = o_small + 2 * GDN_HEADS
    d = w_in.shape[0]
    main = jnp.concatenate([w_in[:, o_ga:], w_in[:, :o_small]], axis=1).astype(BF16)
    small = jnp.zeros((d, HEAD_W), F32).at[:, :2 * GDN_HEADS].set(w_in[:, o_small:o_ga]).astype(BF16)
    return main, small


def kernel(x, positions, norm_mix_w, w_in, q_norm_w, k_norm_w, lam_vec, subln_w, conv_w, a_log, dt_bias,
           gdn_norm_w, w_branch_a, w_branch_b, w_out, norm_ffn_w, ffn_w_gate_up, ffn_w_down, router_w,
           router_b, moe_w_gate_up, moe_w_down):
    batch, seq, d = x.shape
    depth = w_in.shape[0]
    t = batch * seq
    xt = x.reshape(t, d).astype(F32)
    rope_c, rope_s1, rope_s2 = _rope_coefficients(positions)
    h = rmsnorm(xt, norm_mix_w[0].astype(F32))
    for layer in range(depth):
        lam_init = 0.8 - 0.6 * math.exp(-0.3 * layer)
        is_moe = layer % 2 == 1
        j = layer // 2
        w_main, w_small = _split_w_in(w_in[layer])
        proj, small = in_proj(h, w_main, w_small)
        qs, kk, vx = attn_prep(proj, rope_c, rope_s1, rope_s2, q_norm_w[layer], k_norm_w[layer])
        oa = flash_diff_attention(qs, kk, vx, q_norm_w[layer], k_norm_w[layer], lam_vec[layer], subln_w[layer],
                                  lam_init, batch, seq)
        ob = gated_delta_net(proj, small, conv_w[layer], a_log[layer], dt_bias[layer], gdn_norm_w[layer], batch, seq)
        router = (router_w[j], router_b[j]) if is_moe else None
        outs = mixer_out(oa, ob, proj, xt, w_branch_a[layer].astype(BF16), w_branch_b[layer].astype(BF16),
                         w_out[layer].astype(BF16), norm_ffn_w[layer], router)
        if is_moe:
            xt, hn, route = outs
            xt = moe_layer(hn, xt, route, moe_w_gate_up[j].astype(BF16), moe_w_down[j].astype(BF16))
            if layer + 1 < depth:
                h = rmsnorm(xt, norm_mix_w[layer + 1].astype(F32))
        else:
            xt, hn = outs
            nxt = norm_mix_w[layer + 1] if layer + 1 < depth else None
            res = dense_ffn(hn, xt, ffn_w_gate_up[j].astype(BF16), ffn_w_down[j].astype(BF16), nxt)
            if nxt is not None:
                xt, h = res
            else:
                xt = res[0]
    return xt.reshape(batch, seq, d)
```

```python
import functools
import math

import jax
import jax.numpy as jnp
from jax import lax
from jax.experimental import pallas as pl
from jax.experimental.pallas import tpu as pltpu

F32 = jnp.float32
BF16 = jnp.bfloat16

ATT_HEADS = 4
ATT_DIM = 64
ROT_DIM = ATT_DIM // 4
ROPE_THETA = 500000.0
GDN_HEADS = 4
GDN_DK = 128
CONV_K = 4
N_EXPERTS = 8
NORM_EPS = 1e-6

HEAD_W = 128
GROUP_W = 512
COL_GA, COL_GB, COL_AQ, COL_AK, COL_AV, COL_BQ, COL_BK, COL_BV, COL_BZ = 0, 2, 4, 5, 6, 7, 8, 9, 10
N_GROUPS = 11

VMEM_LIMIT = 56 * 1024 * 1024

GDN_CHUNK = 256
GDN_STREAMS = 2
ATT_BLOCK = 1024
ATT_CHUNK = 256
LOG2E = math.log2(math.e)
BOUND_SLACK = 1.01
MAX_SAFE_LOG2_BOUND = 56.0
MOE_TM = 896
MOE_TC = 512
WAIT_RUN = 16
SUBLANES = 8


def _cparams(sem):
    return pltpu.CompilerParams(dimension_semantics=sem, vmem_limit_bytes=VMEM_LIMIT)


def _bdot(a, b):
    return jnp.dot(a.astype(BF16), b.astype(BF16), preferred_element_type=F32)


def _bdot_nt(a, b):
    return lax.dot_general(a.astype(BF16), b.astype(BF16), (((1,), (1,)), ((), ())),
                           preferred_element_type=F32)


def _bdot_tn(a, b):
    return lax.dot_general(a.astype(BF16), b.astype(BF16), (((0,), (0,)), ((), ())),
                           preferred_element_type=F32)


def _hdot(a, b):
    return jnp.dot(a, b, preferred_element_type=F32, precision=lax.Precision.HIGHEST)


def _rms(x, w):
    return x * lax.rsqrt(jnp.mean(x * x, axis=-1, keepdims=True) + NORM_EPS) * w


def _in_proj_body(*refs, n_col_steps, with_norm):
    if with_norm:
        x_ref, nw_ref, w_ref, ws_ref, o_ref, os_ref = refs
        h = _rms(x_ref[...], nw_ref[...]).astype(BF16)
    else:
        h_ref, w_ref, ws_ref, o_ref, os_ref = refs
        h = h_ref[...]
    tn = w_ref.shape[1] // n_col_steps
    for j in range(n_col_steps):
        o_ref[:, j * tn:(j + 1) * tn] = jnp.dot(h, w_ref[:, j * tn:(j + 1) * tn], preferred_element_type=F32)
    os_ref[...] = jnp.dot(h, ws_ref[...], preferred_element_type=F32)


def in_proj(h, w_main, w_small, norm_w=None, tm=512, n_col_steps=2):
    t, d = h.shape
    n = w_main.shape[1]
    const = lambda shape: pl.BlockSpec(shape, lambda i: (0, 0), pipeline_mode=pl.Buffered(1))
    in_specs = [pl.BlockSpec((tm, d), lambda i: (i, 0))]
    args = [h]
    if norm_w is not None:
        in_specs.append(pl.BlockSpec((1, d), lambda i: (0, 0)))
        args.append(norm_w.astype(F32).reshape(1, d))
    return pl.pallas_call(
        functools.partial(_in_proj_body, n_col_steps=n_col_steps, with_norm=norm_w is not None),
        grid=(t // tm,),
        in_specs=in_specs + [const((d, n)), const((d, HEAD_W))],
        out_specs=[pl.BlockSpec((tm, n), lambda i: (i, 0)), pl.BlockSpec((tm, HEAD_W), lambda i: (i, 0))],
        out_shape=[jax.ShapeDtypeStruct((t, n), F32), jax.ShapeDtypeStruct((t, HEAD_W), F32)],
        compiler_params=_cparams(("parallel",)),
        name="in_proj",
    )(*args, w_main, w_small)


def _group_mean_sq(x, m_ref):
    ss = x * x
    hi = ss.astype(BF16)
    lo = (ss - hi.astype(F32)).astype(BF16)
    m = m_ref[...]
    gs = jnp.dot(hi, m, preferred_element_type=F32) + jnp.dot(lo, m, preferred_element_type=F32)
    return gs * (1.0 / ATT_DIM)


def _attn_prep_body(q_ref, k_ref, v_ref, c_ref, s1_ref, s2_ref, qw_ref, kw_ref, m_ref,
                    qs_ref, ko_ref, vx_ref):
    width = q_ref.shape[1]
    reps = width // HEAD_W
    c = jnp.concatenate([c_ref[...]] * reps, axis=1)
    s1 = jnp.concatenate([s1_ref[...]] * reps, axis=1)
    s2 = jnp.concatenate([s2_ref[...]] * reps, axis=1)

    def norm_rope(x, w):
        y = x * lax.rsqrt(_group_mean_sq(x, m_ref) + NORM_EPS) * w
        half = ROT_DIM // 2
        return y * c + pltpu.roll(y, width - half, 1) * s1 + pltpu.roll(y, half, 1) * s2

    q = norm_rope(q_ref[...], qw_ref[...]) * (ATT_DIM ** -0.5 * LOG2E)
    lane = lax.broadcasted_iota(jnp.int32, q.shape, 1)
    first_map = (lane % HEAD_W) < ATT_DIM
    qs_ref[0] = jnp.where(first_map, q, 0.0).astype(BF16)
    qs_ref[1] = jnp.where(first_map, 0.0, q).astype(BF16)
    ko_ref[...] = norm_rope(k_ref[...], kw_ref[...]).astype(BF16)
    v = v_ref[...].astype(BF16)
    ones = jnp.ones((v.shape[0], HEAD_W), BF16)
    parts = []
    for h in range(reps):
        parts += [v[:, h * HEAD_W:(h + 1) * HEAD_W], ones]
    vx_ref[...] = jnp.concatenate(parts, axis=1)


def attn_prep(proj, rope_c, rope_s1, rope_s2, q_norm_w, k_norm_w, tm=512):
    t = proj.shape[0]
    w = GROUP_W
    grp = jnp.arange(w) // ATT_DIM
    ones_bd = (grp[:, None] == grp[None, :]).astype(BF16)
    qw = jnp.tile(q_norm_w.astype(F32), w // ATT_DIM).reshape(1, w)
    kw = jnp.tile(k_norm_w.astype(F32), w // ATT_DIM).reshape(1, w)
    col = lambda g: pl.BlockSpec((tm, w), lambda i, g=g: (i, g))
    tab = pl.BlockSpec((tm, HEAD_W), lambda i: (i, 0))
    const = lambda shape: pl.BlockSpec(shape, lambda i: (0, 0))
    return pl.pallas_call(
        _attn_prep_body,
        grid=(t // tm,),
        in_specs=[col(COL_AQ), col(COL_AK), col(COL_AV), tab, tab, tab,
                  const((1, w)), const((1, w)), const((w, w))],
        out_specs=[pl.BlockSpec((2, tm, w), lambda i: (0, i, 0)), pl.BlockSpec((tm, w), lambda i: (i, 0)),
                   pl.BlockSpec((tm, 2 * w), lambda i: (i, 0))],
        out_shape=[jax.ShapeDtypeStruct((2, t, w), BF16), jax.ShapeDtypeStruct((t, w), BF16),
                   jax.ShapeDtypeStruct((t, 2 * w), BF16)],
        compiler_params=_cparams(("parallel",)),
        name="attn_prep",
    )(proj, proj, proj, rope_c, rope_s1, rope_s2, qw, kw, ones_bd)


def _flash_body(qi_ref, ki_ref, fast_ref, bound_ref, q_ref, k_ref, vx_ref, lam_ref, sw_ref, o_ref,
                m_ref, acc_ref, *, lam_init, chunk):
    p_idx = pl.program_id(1)
    qi = qi_ref[p_idx]
    ki = ki_ref[p_idx]
    tq = q_ref.shape[1]
    tk = k_ref.shape[0]
    fast = fast_ref[0] == 1
    diag = ki == qi

    @pl.when(ki == 0)
    def _():
        m_ref[...] = jnp.full(m_ref.shape, -jnp.inf, F32)
        acc_ref[...] = jnp.zeros(acc_ref.shape, F32)

    def keep_mask(c0, width):
        row = lax.broadcasted_iota(jnp.int32, (2 * tq, width), 0)
        col = lax.broadcasted_iota(jnp.int32, (2 * tq, width), 1) + c0
        return col <= jnp.where(row >= tq, row - tq, row)

    def head_operands(h):
        qk_lanes = pl.ds(pl.multiple_of(h * HEAD_W, HEAD_W), HEAD_W)
        v_lanes = pl.ds(pl.multiple_of(h * 2 * HEAD_W, 2 * HEAD_W), 2 * HEAD_W)
        return q_ref[:, :, qk_lanes].reshape(2 * tq, HEAD_W), qk_lanes, v_lanes

    def update_bounded(h, masked):
        q2, qk_lanes, v_lanes = head_operands(h)
        bound = bound_ref[0]
        for c0 in range(0, tk, chunk):
            k_c = k_ref[c0:c0 + chunk, qk_lanes]
            v_c = vx_ref[c0:c0 + chunk, v_lanes]
            if not masked:
                s = lax.dot_general(q2, k_c, (((1,), (1,)), ((), ())), preferred_element_type=F32)
                acc_ref[h] += jnp.dot(jnp.exp2(s - bound).astype(BF16), v_c, preferred_element_type=F32)
                continue
            rows = tq - c0
            q_low = q_ref[:, c0:, qk_lanes].reshape(2 * rows, HEAD_W)
            s = lax.dot_general(q_low, k_c, (((1,), (1,)), ((), ())), preferred_element_type=F32)
            row = lax.broadcasted_iota(jnp.int32, (2 * rows, chunk), 0)
            col = lax.broadcasted_iota(jnp.int32, (2 * rows, chunk), 1)
            keep = col <= jnp.where(row >= rows, row - rows, row)
            p = jnp.where(keep, jnp.exp2(s - bound), 0.0).astype(BF16)
            part = jnp.dot(p, v_c, preferred_element_type=F32)
            acc_ref[h, c0:tq] += part[:rows]
            acc_ref[h, tq + c0:] += part[rows:]

    def update_running_max(h, masked):
        q2, qk_lanes, v_lanes = head_operands(h)
        s = lax.dot_general(q2, k_ref[:, qk_lanes], (((1,), (1,)), ((), ())), preferred_element_type=F32)
        if masked:
            s = jnp.where(keep_mask(0, tk), s, -jnp.inf)
        m_prev = m_ref[h]
        m_new = jnp.maximum(m_prev, jnp.max(s, axis=-1, keepdims=True))
        p = jnp.exp2(s - m_new)
        acc_ref[h] = (jnp.exp2(m_prev - m_new) * acc_ref[h]
                      + jnp.dot(p.astype(BF16), vx_ref[:, v_lanes], preferred_element_type=F32))
        m_ref[h] = m_new

    def all_heads(update, masked):
        def body(h, carry):
            update(h, masked)
            return carry
        lax.fori_loop(0, ATT_HEADS, body, 0)

    for use_fast, update in ((True, update_bounded), (False, update_running_max)):
        path = fast if use_fast else jnp.logical_not(fast)

        @pl.when(jnp.logical_and(path, jnp.logical_not(diag)))
        def _(update=update):
            all_heads(update, False)

        @pl.when(jnp.logical_and(path, diag))
        def _(update=update):
            all_heads(update, True)

    @pl.when(diag)
    def _():
        lv = lam_ref[...]
        lam = (jnp.exp(jnp.sum(lv[0:1] * lv[1:2], axis=-1, keepdims=True))
               - jnp.exp(jnp.sum(lv[2:3] * lv[3:4], axis=-1, keepdims=True)) + lam_init)
        for h in range(ATT_HEADS):
            o = (acc_ref[h, 0:tq, 0:HEAD_W] / acc_ref[h, 0:tq, HEAD_W:]
                 - lam * (acc_ref[h, tq:, 0:HEAD_W] / acc_ref[h, tq:, HEAD_W:]))
            o_ref[:, h * HEAD_W:(h + 1) * HEAD_W] = (_rms(o, sw_ref[...]) * (1.0 - lam_init)).astype(o_ref.dtype)


def flash_diff_attention(qs, k, vx, q_norm_w, k_norm_w, lam_vec, subln_w, lam_init, batch, seq,
                         blk=ATT_BLOCK, chunk=ATT_CHUNK):
    t = k.shape[0]
    nq = seq // blk
    w = ATT_HEADS * HEAD_W
    pairs = [(i, j) for i in range(nq) for j in range(i + 1)]
    qi_tab = jnp.asarray([p[0] for p in pairs], jnp.int32)
    ki_tab = jnp.asarray([p[1] for p in pairs], jnp.int32)
    bound = (ATT_DIM ** 0.5) * jnp.max(jnp.abs(q_norm_w.astype(F32))) * jnp.max(jnp.abs(k_norm_w.astype(F32)))
    bound = bound * (LOG2E * BOUND_SLACK)
    fast = (bound <= MAX_SAFE_LOG2_BOUND).astype(jnp.int32).reshape(1)
    idx = lambda f: (lambda b, p, qi, ki, fast: f(b, qi[p], ki[p]))
    grid_spec = pltpu.PrefetchScalarGridSpec(
        num_scalar_prefetch=3,
        grid=(batch, len(pairs)),
        in_specs=[pl.BlockSpec(memory_space=pltpu.SMEM),
                  pl.BlockSpec((2, blk, w), idx(lambda b, qi, ki: (0, b * nq + qi, 0))),
                  pl.BlockSpec((blk, w), idx(lambda b, qi, ki: (b * nq + ki, 0))),
                  pl.BlockSpec((blk, 2 * w), idx(lambda b, qi, ki: (b * nq + ki, 0))),
                  pl.BlockSpec((4, ATT_DIM), idx(lambda b, qi, ki: (0, 0))),
                  pl.BlockSpec((1, HEAD_W), idx(lambda b, qi, ki: (0, 0)))],
        out_specs=pl.BlockSpec((blk, w), idx(lambda b, qi, ki: (b * nq + qi, 0))),
        scratch_shapes=[pltpu.VMEM((ATT_HEADS, 2 * blk, 1), F32),
                        pltpu.VMEM((ATT_HEADS, 2 * blk, 2 * HEAD_W), F32)],
    )
    return pl.pallas_call(
        functools.partial(_flash_body, lam_init=lam_init, chunk=chunk),
        grid_spec=grid_spec,
        out_shape=jax.ShapeDtypeStruct((t, w), BF16),
        compiler_params=_cparams(("parallel", "arbitrary")),
        name="flash_diff_attention",
    )(qi_tab, ki_tab, fast, bound.reshape(1), qs, k, vx, lam_vec.astype(F32),
      subln_w.astype(F32).reshape(1, HEAD_W))


def _softplus(x):
    return jnp.maximum(x, 0.0) + jnp.log1p(jnp.exp(-jnp.abs(x)))


def _conv_silu(x, halo, w):
    head = x[0:8]
    row8 = lax.broadcasted_iota(jnp.int32, head.shape, 0)
    y = x * w[CONV_K - 1:CONV_K]
    y_head = head * w[CONV_K - 1:CONV_K]
    for j in range(1, CONV_K):
        wj = w[CONV_K - 1 - j:CONV_K - j]
        y = y + pltpu.roll(x, j, 0) * wj
        shifted = jnp.where(row8 < j, pltpu.roll(halo, j, 0), pltpu.roll(head, j, 0))
        y_head = y_head + shifted * wj
    y = jnp.concatenate([y_head, y[8:]], axis=0)
    return y * jax.nn.sigmoid(y)


def _l2n(x):
    return x * lax.rsqrt(jnp.sum(x * x, axis=-1, keepdims=True) + NORM_EPS)


def _gdn_body(q_ref, k_ref, v_ref, hq_ref, hk_ref, hv_ref, sm_ref, cq_ref, ck_ref, cv_ref, al_ref, dt_ref,
              z_ref, nw_ref, o_ref, state_ref):
    n_streams, c = q_ref.shape[0], q_ref.shape[1]
    heads = range(GDN_HEADS)
    items = [(s, h) for s in range(n_streams) for h in heads]
    seq_start = pl.program_id(1) == 0

    @pl.when(seq_start)
    def _():
        state_ref[...] = jnp.zeros(state_ref.shape, F32)

    def conv(x_ref, halo_ref, w_ref, s):
        return _conv_silu(x_ref[s], jnp.where(seq_start, 0.0, halo_ref[s]), w_ref[...])

    row = lax.broadcasted_iota(jnp.int32, (c, c), 0)
    col = lax.broadcasted_iota(jnp.int32, (c, c), 1)
    incl = row >= col
    strict = row > col
    xor_idx = row ^ col
    lower_ones = jnp.where(incl, 1.0, 0.0).astype(F32)
    upper_ones = jnp.where(row <= col, 1.0, 0.0).astype(F32)
    sl = [slice(h * HEAD_W, (h + 1) * HEAD_W) for h in heads]

    q, k, v_beta, beta, gcol, grow = {}, {}, {}, {}, {}, {}
    for s in range(n_streams):
        q_all = conv(q_ref, hq_ref, cq_ref, s)
        k_all = conv(k_ref, hk_ref, ck_ref, s)
        v_all = conv(v_ref, hv_ref, cv_ref, s)
        raw = sm_ref[s]
        lane = lax.broadcasted_iota(jnp.int32, raw.shape, 1)
        sm = jnp.where(lane < GDN_HEADS, jax.nn.sigmoid(raw),
                       jnp.where(lane < 2 * GDN_HEADS, -jnp.exp(al_ref[...]) * _softplus(raw + dt_ref[...]), 0.0))
        gc_col = _hdot(lower_ones, sm)
        gc_row = _hdot(sm.T[0:8], upper_ones)
        for h in heads:
            q[s, h] = _l2n(q_all[:, sl[h]]) * (GDN_DK ** -0.5)
            k[s, h] = _l2n(k_all[:, sl[h]])
            beta[s, h] = sm[:, h:h + 1]
            v_beta[s, h] = v_all[:, sl[h]] * beta[s, h]
            gcol[s, h] = gc_col[:, GDN_HEADS + h:GDN_HEADS + h + 1]
            grow[s, h] = gc_row[GDN_HEADS + h:GDN_HEADS + h + 1, :]

    decay = {i: jnp.where(incl, jnp.exp(jnp.minimum(gcol[i] - grow[i], 0.0)), 0.0) for i in items}
    kb = {i: k[i] * beta[i] for i in items}
    l_mat = {i: jnp.where(strict, _bdot_nt(kb[i], k[i]) * decay[i], 0.0) for i in items}
    a_qk = {i: _bdot_nt(q[i], k[i]) * decay[i] for i in items}
    eg = {i: jnp.exp(gcol[i]) for i in items}
    rhs = {i: jnp.concatenate([v_beta[i], kb[i] * eg[i]], axis=1) for i in items}
    eye = jnp.where(row == col, 1.0, 0.0).astype(F32)
    x = {i: eye for i in items}
    b = 1
    while b < c:
        level = (xor_idx >= b) & (xor_idx < 2 * b)
        ex = {i: _bdot(jnp.where(level, l_mat[i], 0.0), x[i]) for i in items}
        x = {i: x[i] - _bdot(x[i], ex[i]) for i in items}
        b *= 2
    sol = {i: _bdot(x[i], rhs[i]) for i in items}
    state = {(s, h): state_ref[s * GDN_HEADS + h] for s, h in items}
    v_new = {i: sol[i][:, :HEAD_W] - _bdot(sol[i][:, HEAD_W:], state[i]) for i in items}
    o = {i: _bdot(q[i] * eg[i], state[i]) + _bdot(a_qk[i], v_new[i]) for i in items}
    nw = nw_ref[...]
    for s, h in items:
        i = (s, h)
        g_last = gcol[i][c - 1:c, :]
        k_dec = k[i] * jnp.exp(g_last - gcol[i])
        state_ref[s * GDN_HEADS + h] = state[i] * jnp.exp(g_last) + _bdot_tn(k_dec, v_new[i])
        z = z_ref[s, :, sl[h]]
        o_ref[s, :, sl[h]] = (_rms(o[i], nw) * (z * jax.nn.sigmoid(z))).astype(o_ref.dtype)


def gated_delta_net(proj, small, conv_w, a_log, dt_bias, gdn_norm_w, batch, seq, chunk=GDN_CHUNK):
    t = proj.shape[0]
    nc = seq // chunk
    w = GROUP_W
    n_streams = GDN_STREAMS if batch % GDN_STREAMS == 0 else 1
    groups = batch // n_streams
    proj3 = proj.reshape(n_streams, t // n_streams, proj.shape[1])
    small3 = small.reshape(n_streams, t // n_streams, HEAD_W)
    cw = conv_w.astype(F32)
    pad = lambda vec: jnp.zeros((1, HEAD_W), F32).at[0, GDN_HEADS:2 * GDN_HEADS].set(vec.astype(F32))
    col = lambda g: pl.BlockSpec((n_streams, chunk, w), lambda b, c, g=g: (0, b * nc + c, g))
    halo = lambda g: pl.BlockSpec((n_streams, 8, w),
                                  lambda b, c, g=g: (0, jnp.maximum((b * nc + c) * (chunk // 8) - 1, 0), g))
    const = lambda shape: pl.BlockSpec(shape, lambda b, c: (0, 0))
    out = pl.pallas_call(
        _gdn_body,
        grid=(groups, nc),
        in_specs=[col(COL_BQ), col(COL_BK), col(COL_BV), halo(COL_BQ), halo(COL_BK), halo(COL_BV),
                  pl.BlockSpec((n_streams, chunk, HEAD_W), lambda b, c: (0, b * nc + c, 0)),
                  const((CONV_K, w)), const((CONV_K, w)), const((CONV_K, w)),
                  const((1, HEAD_W)), const((1, HEAD_W)), col(COL_BZ), const((1, HEAD_W))],
        out_specs=pl.BlockSpec((n_streams, chunk, w), lambda b, c: (0, b * nc + c, 0)),
        out_shape=jax.ShapeDtypeStruct((n_streams, t // n_streams, w), BF16),
        scratch_shapes=[pltpu.VMEM((n_streams * GDN_HEADS, GDN_DK, HEAD_W), F32)],
        compiler_params=_cparams(("parallel", "arbitrary")),
        name="gated_delta_net",
    )(proj3, proj3, proj3, proj3, proj3, proj3, small3, cw[:, 0:w], cw[:, w:2 * w], cw[:, 2 * w:3 * w],
      pad(a_log), pad(dt_bias), proj3, gdn_norm_w.astype(F32).reshape(1, HEAD_W))
    return out.reshape(t, w)


def _top2_route(logits):
    lane = lax.broadcasted_iota(jnp.int32, logits.shape, 1).astype(F32)
    big = float(logits.shape[1])
    m1 = jnp.max(logits, axis=-1, keepdims=True)
    i1 = jnp.min(jnp.where(logits == m1, lane, big), axis=-1, keepdims=True)
    rest = jnp.where(lane == i1, -jnp.inf, logits)
    m2 = jnp.max(rest, axis=-1, keepdims=True)
    i2 = jnp.min(jnp.where(rest == m2, lane, big), axis=-1, keepdims=True)
    e = jnp.exp(m2 - m1)
    g1 = 1.0 / (1.0 + e)
    g2 = e / (1.0 + e)
    return jnp.where(lane == 0, i1, jnp.where(lane == 1, i2, jnp.where(lane == 2, g1, jnp.where(lane == 3, g2, 0.0))))


def _mixer_out_body(*refs, with_router):
    if with_router:
        (oa_ref, ob_ref, ga_ref, gb_ref, x_ref, wa_ref, wb_ref, wo_ref, nw_ref, rwh_ref, rwl_ref, rb_ref,
         xo_ref, ho_ref, ro_ref) = refs
    else:
        (oa_ref, ob_ref, ga_ref, gb_ref, x_ref, wa_ref, wb_ref, wo_ref, nw_ref, xo_ref, ho_ref) = refs
    ya = jnp.dot(oa_ref[...], wa_ref[...], preferred_element_type=F32)
    yb = jnp.dot(ob_ref[...], wb_ref[...], preferred_element_type=F32)
    merged = jax.nn.sigmoid(ga_ref[...]) * ya + jax.nn.sigmoid(gb_ref[...]) * yb
    x_new = x_ref[...] + jnp.dot(merged.astype(BF16), wo_ref[...], preferred_element_type=F32)
    xo_ref[...] = x_new
    hn = _rms(x_new, nw_ref[...])
    ho_ref[...] = hn.astype(ho_ref.dtype)
    if with_router:
        hn_hi = hn.astype(BF16)
        hn_lo = (hn - hn_hi.astype(F32)).astype(BF16)
        rw_hi = rwh_ref[...]
        logits = (jnp.dot(hn_hi, rw_hi, preferred_element_type=F32)
                  + jnp.dot(hn_lo, rw_hi, preferred_element_type=F32)
                  + jnp.dot(hn_hi, rwl_ref[...], preferred_element_type=F32))
        ro_ref[...] = _top2_route(logits + rb_ref[...])


def mixer_out(oa, ob, proj, x, w_a, w_b, w_o, norm_w, router=None, tm=512):
    t, d = x.shape
    w = GROUP_W
    row = lambda width, g=0: pl.BlockSpec((tm, width), lambda i, g=g: (i, g))
    const = lambda shape: pl.BlockSpec(shape, lambda i: (0, 0))
    in_specs = [row(w), row(w), row(d, COL_GA * w // d), row(d, COL_GB * w // d), row(d),
                const((w, d)), const((w, d)), const((d, d)), const((1, d))]
    args = [oa, ob, proj, proj, x, w_a, w_b, w_o, norm_w.astype(F32).reshape(1, d)]
    out_specs = [row(d), row(d)]
    out_shape = [jax.ShapeDtypeStruct((t, d), F32),
                 jax.ShapeDtypeStruct((t, d), F32 if router is not None else BF16)]
    if router is not None:
        rw, rb = router
        rw_pad = jnp.zeros((d, HEAD_W), F32).at[:, :N_EXPERTS].set(rw.astype(F32))
        rw_hi = rw_pad.astype(BF16)
        rw_lo = (rw_pad - rw_hi.astype(F32)).astype(BF16)
        rb_pad = jnp.full((1, HEAD_W), -jnp.inf, F32).at[0, :N_EXPERTS].set(rb.astype(F32))
        in_specs += [const((d, HEAD_W)), const((d, HEAD_W)), const((1, HEAD_W))]
        args += [rw_hi, rw_lo, rb_pad]
        out_specs.append(row(HEAD_W))
        out_shape.append(jax.ShapeDtypeStruct((t, HEAD_W), F32))
    return pl.pallas_call(
        functools.partial(_mixer_out_body, with_router=router is not None),
        grid=(t // tm,),
        in_specs=in_specs,
        out_specs=out_specs,
        out_shape=out_shape,
        compiler_params=_cparams(("parallel",)),
        name="mixer_out",
    )(*args)


def _ffn_body(*refs, with_norm, tc):
    if with_norm:
        h_ref, x_ref, wgu_ref, wd_ref, nw_ref, xo_ref, ho_ref, acc_ref = refs
    else:
        h_ref, x_ref, wgu_ref, wd_ref, xo_ref, acc_ref = refs
    h = h_ref[...]
    d_ff = wd_ref.shape[0]
    for ci, c0 in enumerate(range(0, d_ff, tc)):
        g = jnp.dot(h, wgu_ref[:, c0:c0 + tc], preferred_element_type=F32)
        u = jnp.dot(h, wgu_ref[:, d_ff + c0:d_ff + c0 + tc], preferred_element_type=F32)
        a = (g * jax.nn.sigmoid(g) * u).astype(BF16)
        part = jnp.dot(a, wd_ref[c0:c0 + tc, :], preferred_element_type=F32)
        if ci == 0:
            acc_ref[...] = x_ref[...] + part
        else:
            acc_ref[...] += part
    x_new = acc_ref[...]
    xo_ref[...] = x_new
    if with_norm:
        ho_ref[...] = _rms(x_new, nw_ref[...]).astype(ho_ref.dtype)


def dense_ffn(h, x, w_gu, w_d, next_norm_w=None, tm=512, tc=256):
    t, d = x.shape
    d_ff = w_d.shape[0]
    row = pl.BlockSpec((tm, d), lambda i: (i, 0))
    const = lambda shape: pl.BlockSpec(shape, lambda i: (0, 0), pipeline_mode=pl.Buffered(1))
    in_specs = [row, row, const((d, 2 * d_ff)), const((d_ff, d))]
    args = [h, x, w_gu, w_d]
    out_specs = [row]
    out_shape = [jax.ShapeDtypeStruct((t, d), F32)]
    if next_norm_w is not None:
        in_specs.append(pl.BlockSpec((1, d), lambda i: (0, 0)))
        args.append(next_norm_w.astype(F32).reshape(1, d))
        out_specs.append(row)
        out_shape.append(jax.ShapeDtypeStruct((t, d), BF16))
    return pl.pallas_call(
        functools.partial(_ffn_body, with_norm=next_norm_w is not None, tc=tc),
        grid=(t // tm,),
        in_specs=in_specs,
        out_specs=out_specs,
        out_shape=out_shape,
        scratch_shapes=[pltpu.VMEM((tm, d), F32)],
        compiler_params=_cparams(("parallel",)),
        name="dense_ffn",
    )(*args)


def _expert_body(be_ref, tok_ref, nl_ref, h_ref, wg_ref, wu_ref, wd_ref, o_ref, xbuf_ref, xb_ref, acc_ref, sem,
                 *, n_c):
    del be_ref
    blk = pl.program_id(0)
    c = pl.program_id(1)
    n_blk = pl.num_programs(0)
    tm = xb_ref.shape[0]
    slot = blk % 2
    live = blk < nl_ref[0]
    per_step = tm // n_c

    def row_copy(b, grp, i, s):
        tok = tok_ref[b * tm + grp * SUBLANES + i]
        return pltpu.make_async_copy(h_ref.at[pl.ds(tok, 1)], xbuf_ref.at[s, grp, pl.ds(i, 1)], sem.at[s])

    def wait_block(s):
        def body(g, carry):
            for j in range(WAIT_RUN // SUBLANES):
                for i in range(SUBLANES):
                    dst = xbuf_ref.at[s, g * (WAIT_RUN // SUBLANES) + j, pl.ds(i, 1)]
                    pltpu.make_async_copy(h_ref.at[pl.ds(0, 1)], dst, sem.at[s]).wait()
            return carry
        lax.fori_loop(0, tm // WAIT_RUN, body, 0)

    @pl.when(jnp.logical_and(blk == 0, c == 0))
    def _():
        def body(grp, carry):
            for i in range(SUBLANES):
                row_copy(0, grp, i, 0).start()
            return carry
        lax.fori_loop(0, tm // SUBLANES, body, 0)

    @pl.when(c == 0)
    def _():
        wait_block(slot)
        xb_ref[...] = xbuf_ref[slot].reshape(tm, xb_ref.shape[1]).astype(BF16)
        acc_ref[...] = jnp.zeros(acc_ref.shape, F32)

    nxt = jnp.where(blk + 1 < n_blk, blk + 1, 0)

    def gather_next():
        for j in range(per_step // SUBLANES):
            for i in range(SUBLANES):
                row_copy(nxt, c * (per_step // SUBLANES) + j, i, 1 - slot).start()

    @pl.when(live)
    def _():
        gather_next()
        xb = xb_ref[...]
        g = jnp.dot(xb, wg_ref[0], preferred_element_type=F32)
        u = jnp.dot(xb, wu_ref[0], preferred_element_type=F32)
        a = (g * jax.nn.sigmoid(g) * u).astype(BF16)
        acc_ref[...] += jnp.dot(a, wd_ref[0], preferred_element_type=F32)

    @pl.when(jnp.logical_not(live))
    def _():
        gather_next()

    @pl.when(c == n_c - 1)
    def _():
        o_ref[...] = acc_ref[...]

    @pl.when(jnp.logical_and(blk == n_blk - 1, c == n_c - 1))
    def _():
        wait_block(1 - slot)


def moe_experts(h, slot_tok, block_e, n_live, w_gu, w_d, tm=MOE_TM, tc=MOE_TC):
    n_slots = slot_tok.shape[0]
    d = h.shape[1]
    d_e = w_d.shape[1]
    n_c = d_e // tc
    assert tm % (n_c * SUBLANES) == 0, "each grid step gathers an equal share of the next block's row groups"
    assert tm % WAIT_RUN == 0 and WAIT_RUN % SUBLANES == 0
    grid_spec = pltpu.PrefetchScalarGridSpec(
        num_scalar_prefetch=3,
        grid=(n_slots // tm, n_c),
        in_specs=[pl.BlockSpec(memory_space=pl.ANY),
                  pl.BlockSpec((1, d, tc), lambda b, c, be, tok, nl: (be[b], 0, c)),
                  pl.BlockSpec((1, d, tc), lambda b, c, be, tok, nl: (be[b], 0, n_c + c)),
                  pl.BlockSpec((1, tc, d), lambda b, c, be, tok, nl: (be[b], c, 0))],
        out_specs=pl.BlockSpec((tm, d), lambda b, c, be, tok, nl: (b, 0)),
        scratch_shapes=[pltpu.VMEM((2, tm // SUBLANES, SUBLANES, d), F32), pltpu.VMEM((tm, d), BF16),
                        pltpu.VMEM((tm, d), F32), pltpu.SemaphoreType.DMA((2,))],
    )
    return pl.pallas_call(
        functools.partial(_expert_body, n_c=n_c),
        grid_spec=grid_spec,
        out_shape=jax.ShapeDtypeStruct((n_slots, d), F32),
        compiler_params=_cparams(("arbitrary", "arbitrary")),
        name="moe_experts",
    )(block_e, slot_tok, n_live, h, w_gu, w_gu, w_d)


def _combine_body(dest_ref, x_ref, r_ref, yb_ref, o_ref, buf_ref, sem, *, tb):
    step = pl.program_id(0)
    slot = step % 2

    def issue(st, s):
        def body(g, carry):
            for i in range(SUBLANES):
                for k in range(2):
                    src = yb_ref.at[pl.ds(dest_ref[2 * (st * tb + g * SUBLANES + i) + k], 1)]
                    pltpu.make_async_copy(src, buf_ref.at[s, k, g, pl.ds(i, 1)], sem.at[s]).start()
            return carry
        lax.fori_loop(0, tb // SUBLANES, body, 0)

    def wait_all(s):
        def body(g, carry):
            for j in range(WAIT_RUN // SUBLANES):
                for i in range(SUBLANES):
                    for k in range(2):
                        dst = buf_ref.at[s, k, g * (WAIT_RUN // SUBLANES) + j, pl.ds(i, 1)]
                        pltpu.make_async_copy(yb_ref.at[pl.ds(0, 1)], dst, sem.at[s]).wait()
            return carry
        lax.fori_loop(0, tb // WAIT_RUN, body, 0)

    @pl.when(step == 0)
    def _():
        issue(0, 0)

    @pl.when(step + 1 < pl.num_programs(0))
    def _():
        issue(step + 1, 1 - slot)

    wait_all(slot)
    route = r_ref[...]
    y0 = buf_ref[slot, 0].reshape(o_ref.shape)
    y1 = buf_ref[slot, 1].reshape(o_ref.shape)
    o_ref[...] = x_ref[...] + route[:, 2:3] * y0 + route[:, 3:4] * y1


def moe_combine(x, route, yb, dest, tb=256):
    t, d = x.shape
    assert tb % WAIT_RUN == 0
    grid_spec = pltpu.PrefetchScalarGridSpec(
        num_scalar_prefetch=1,
        grid=(t // tb,),
        in_specs=[pl.BlockSpec((tb, d), lambda i, dest: (i, 0)),
                  pl.BlockSpec((tb, HEAD_W), lambda i, dest: (i, 0)),
                  pl.BlockSpec(memory_space=pl.ANY)],
        out_specs=pl.BlockSpec((tb, d), lambda i, dest: (i, 0)),
        scratch_shapes=[pltpu.VMEM((2, 2, tb // SUBLANES, SUBLANES, d), F32), pltpu.SemaphoreType.DMA((2,))],
    )
    return pl.pallas_call(
        functools.partial(_combine_body, tb=tb),
        grid_spec=grid_spec,
        out_shape=jax.ShapeDtypeStruct((t, d), F32),
        compiler_params=_cparams(("arbitrary",)),
        name="moe_combine",
    )(dest, x, route, yb)


def moe_layer(hn, x, route, w_gu, w_d, tm=MOE_TM):
    t, d = x.shape
    n_assign = 2 * t
    e_flat = route[:, :2].astype(jnp.int32).reshape(n_assign)
    onehot = (e_flat[:, None] == jnp.arange(N_EXPERTS, dtype=jnp.int32)[None, :]).astype(jnp.int32)
    incl = jnp.cumsum(onehot, axis=0)
    rank = jnp.sum((incl - onehot) * onehot, axis=1)
    counts = incl[-1]
    padded = (counts + tm - 1) // tm * tm
    pad_end = jnp.cumsum(padded)
    pad_start = pad_end - padded
    dest = (pad_start[e_flat] + rank).astype(jnp.int32)
    n_blocks = -(-n_assign // tm) + N_EXPERTS
    block_start = jnp.arange(n_blocks, dtype=jnp.int32) * tm
    block_e = jnp.minimum(jnp.sum((pad_end[None, :] <= block_start[:, None]).astype(jnp.int32), axis=1),
                          N_EXPERTS - 1).astype(jnp.int32)
    slot_tok = jnp.zeros((n_blocks * tm,), jnp.int32).at[dest].set(jnp.arange(n_assign, dtype=jnp.int32) // 2,
                                                                  unique_indices=True)
    n_live = (pad_end[-1:] // tm).astype(jnp.int32)
    yb = moe_experts(hn, slot_tok, block_e, n_live, w_gu, w_d)
    return moe_combine(x, route, yb, dest)


def _rope_coefficients(positions):
    half = ROT_DIM // 2
    lane = jnp.arange(HEAD_W) % ATT_DIM
    inv_freq = ROPE_THETA ** (-(2.0 * (lane % half)).astype(F32) / ROT_DIM)
    freq = jnp.where(lane < ROT_DIM, inv_freq, 0.0)
    ang = positions.astype(F32).reshape(-1, 1) * freq[None, :]
    sin = jnp.sin(ang)
    return (jnp.cos(ang), jnp.where(lane < half, -sin, 0.0),
            jnp.where((lane >= half) & (lane < ROT_DIM), sin, 0.0))


def _split_w_in(w_in):
    att = 3 * GROUP_W
    conv = 3 * GROUP_W
    o_z = att + conv
    o_small = o_z + GROUP_W
    o_ga = o_small + 2 * GDN_HEADS
    d = w_in.shape[0]
    main = jnp.concatenate([w_in[:, o_ga:], w_in[:, :o_small]], axis=1).astype(BF16)
    small = jnp.zeros((d, HEAD_W), F32).at[:, :2 * GDN_HEADS].set(w_in[:, o_small:o_ga]).astype(BF16)
    return main, small


def kernel(x, positions, norm_mix_w, w_in, q_norm_w, k_norm_w, lam_vec, subln_w, conv_w, a_log, dt_bias,
           gdn_norm_w, w_branch_a, w_branch_b, w_out, norm_ffn_w, ffn_w_gate_up, ffn_w_down, router_w,
           router_b, moe_w_gate_up, moe_w_down):
    batch, seq, d = x.shape
    depth = w_in.shape[0]
    t = batch * seq
    xt = x.reshape(t, d).astype(F32)
    rope_c, rope_s1, rope_s2 = _rope_coefficients(positions)
    h = None
    for layer in range(depth):
        lam_init = 0.8 - 0.6 * math.exp(-0.3 * layer)
        is_moe = layer % 2 == 1
        j = layer // 2
        w_main, w_small = _split_w_in(w_in[layer])
        if h is None:
            proj, small = in_proj(xt, w_main, w_small, norm_mix_w[layer])
        else:
            proj, small = in_proj(h, w_main, w_small)
        qs, kk, vx = attn_prep(proj, rope_c, rope_s1, rope_s2, q_norm_w[layer], k_norm_w[layer])
        oa = flash_diff_attention(qs, kk, vx, q_norm_w[layer], k_norm_w[layer], lam_vec[layer], subln_w[layer],
                                  lam_init, batch, seq)
        ob = gated_delta_net(proj, small, conv_w[layer], a_log[layer], dt_bias[layer], gdn_norm_w[layer], batch, seq)
        router = (router_w[j], router_b[j]) if is_moe else None
        outs = mixer_out(oa, ob, proj, xt, w_branch_a[layer].astype(BF16), w_branch_b[layer].astype(BF16),
                         w_out[layer].astype(BF16), norm_ffn_w[layer], router)
        if is_moe:
            xt, hn, route = outs
            xt = moe_layer(hn, xt, route, moe_w_gate_up[j].astype(BF16), moe_w_down[j].astype(BF16))
            h = None
        else:
            xt, hn = outs
            nxt = norm_mix_w[layer + 1] if layer + 1 < depth else None
            res = dense_ffn(hn, xt, ffn_w_gate_up[j].astype(BF16), ffn_w_down[j].astype(BF16), nxt)
            if nxt is not None:
                xt, h = res
            else:
                xt = res[0]
    return xt.reshape(batch, seq, d)
```

```python
import functools
import math

import jax
import jax.numpy as jnp
from jax import lax
from jax.experimental import pallas as pl
from jax.experimental.pallas import tpu as pltpu

F32 = jnp.float32
BF16 = jnp.bfloat16

ATT_HEADS = 4
ATT_DIM = 64
ROT_DIM = ATT_DIM // 4
ROPE_THETA = 500000.0
GDN_HEADS = 4
GDN_DK = 128
CONV_K = 4
N_EXPERTS = 8
NORM_EPS = 1e-6

HEAD_W = 128
GROUP_W = 512
COL_GA, COL_GB, COL_AQ, COL_AK, COL_AV, COL_BQ, COL_BK, COL_BV, COL_BZ = 0, 2, 4, 5, 6, 7, 8, 9, 10
N_GROUPS = 11

VMEM_LIMIT = 56 * 1024 * 1024

GDN_CHUNK = 256
GDN_STREAMS = 2
ATT_BLOCK = 1024
ATT_CHUNK = 256
LOG2E = math.log2(math.e)
BOUND_SLACK = 1.01
MAX_SAFE_LOG2_BOUND = 56.0
MOE_TM = 896
MOE_TC = 512
WAIT_RUN = 16
SUBLANES = 8


def _cparams(sem):
    return pltpu.CompilerParams(dimension_semantics=sem, vmem_limit_bytes=VMEM_LIMIT)


def _bdot(a, b):
    return jnp.dot(a.astype(BF16), b.astype(BF16), preferred_element_type=F32)


def _bdot_nt(a, b):
    return lax.dot_general(a.astype(BF16), b.astype(BF16), (((1,), (1,)), ((), ())),
                           preferred_element_type=F32)


def _bdot_tn(a, b):
    return lax.dot_general(a.astype(BF16), b.astype(BF16), (((0,), (0,)), ((), ())),
                           preferred_element_type=F32)


def _hdot(a, b):
    return jnp.dot(a, b, preferred_element_type=F32, precision=lax.Precision.HIGHEST)


def _rms(x, w):
    return x * lax.rsqrt(jnp.mean(x * x, axis=-1, keepdims=True) + NORM_EPS) * w


def _group_mean_sq(x, m_ref):
    ss = x * x
    hi = ss.astype(BF16)
    lo = (ss - hi.astype(F32)).astype(BF16)
    m = m_ref[...]
    gs = jnp.dot(hi, m, preferred_element_type=F32) + jnp.dot(lo, m, preferred_element_type=F32)
    return gs * (1.0 / ATT_DIM)


def _in_proj_body(*refs, with_norm):
    if with_norm:
        x_ref, nw_ref = refs[:2]
        h = _rms(x_ref[...], nw_ref[...]).astype(BF16)
        refs = refs[2:]
    else:
        h = refs[0][...]
        refs = refs[1:]
    (w_ref, ws_ref, c_ref, s1_ref, s2_ref, qw_ref, kw_ref, m_ref,
     o_ref, os_ref, qs_ref, ko_ref, vx_ref) = refs
    w = GROUP_W
    lo, hi = COL_AQ * w, (COL_AV + 1) * w
    n = w_ref.shape[1]
    o_ref[:, 0:lo] = jnp.dot(h, w_ref[:, 0:lo], preferred_element_type=F32)
    qkv = jnp.dot(h, w_ref[:, lo:hi], preferred_element_type=F32)
    o_ref[:, lo:hi] = qkv
    o_ref[:, hi:n] = jnp.dot(h, w_ref[:, hi:n], preferred_element_type=F32)
    os_ref[...] = jnp.dot(h, ws_ref[...], preferred_element_type=F32)

    reps = w // HEAD_W
    c = jnp.concatenate([c_ref[...]] * reps, axis=1)
    s1 = jnp.concatenate([s1_ref[...]] * reps, axis=1)
    s2 = jnp.concatenate([s2_ref[...]] * reps, axis=1)

    def norm_rope(x, nw):
        y = x * lax.rsqrt(_group_mean_sq(x, m_ref) + NORM_EPS) * nw
        half = ROT_DIM // 2
        return y * c + pltpu.roll(y, w - half, 1) * s1 + pltpu.roll(y, half, 1) * s2

    q = norm_rope(qkv[:, 0:w], qw_ref[...]) * (ATT_DIM ** -0.5 * LOG2E)
    lane = lax.broadcasted_iota(jnp.int32, q.shape, 1)
    first_map = (lane % HEAD_W) < ATT_DIM
    qs_ref[0] = jnp.where(first_map, q, 0.0).astype(BF16)
    qs_ref[1] = jnp.where(first_map, 0.0, q).astype(BF16)
    ko_ref[...] = norm_rope(qkv[:, w:2 * w], kw_ref[...]).astype(BF16)
    v = qkv[:, 2 * w:3 * w].astype(BF16)
    ones = jnp.ones((v.shape[0], HEAD_W), BF16)
    parts = []
    for hh in range(reps):
        parts += [v[:, hh * HEAD_W:(hh + 1) * HEAD_W], ones]
    vx_ref[...] = jnp.concatenate(parts, axis=1)


def in_proj(h, w_main, w_small, rope, q_norm_w, k_norm_w, norm_w=None, tm=512):
    t, d = h.shape
    n = w_main.shape[1]
    w = GROUP_W
    grp = jnp.arange(w) // ATT_DIM
    ones_bd = (grp[:, None] == grp[None, :]).astype(BF16)
    qw = jnp.tile(q_norm_w.astype(F32), w // ATT_DIM).reshape(1, w)
    kw = jnp.tile(k_norm_w.astype(F32), w // ATT_DIM).reshape(1, w)
    const = lambda shape: pl.BlockSpec(shape, lambda i: (0, 0), pipeline_mode=pl.Buffered(1))
    small_const = lambda shape: pl.BlockSpec(shape, lambda i: (0, 0))
    tab = pl.BlockSpec((tm, HEAD_W), lambda i: (i, 0))
    in_specs = [pl.BlockSpec((tm, d), lambda i: (i, 0))]
    args = [h]
    if norm_w is not None:
        in_specs.append(small_const((1, d)))
        args.append(norm_w.astype(F32).reshape(1, d))
    in_specs += [const((d, n)), const((d, HEAD_W)), tab, tab, tab,
                 small_const((1, w)), small_const((1, w)), small_const((w, w))]
    args += [w_main, w_small, *rope, qw, kw, ones_bd]
    return pl.pallas_call(
        functools.partial(_in_proj_body, with_norm=norm_w is not None),
        grid=(t // tm,),
        in_specs=in_specs,
        out_specs=[pl.BlockSpec((tm, n), lambda i: (i, 0)), pl.BlockSpec((tm, HEAD_W), lambda i: (i, 0)),
                   pl.BlockSpec((2, tm, w), lambda i: (0, i, 0)), pl.BlockSpec((tm, w), lambda i: (i, 0)),
                   pl.BlockSpec((tm, 2 * w), lambda i: (i, 0))],
        out_shape=[jax.ShapeDtypeStruct((t, n), F32), jax.ShapeDtypeStruct((t, HEAD_W), F32),
                   jax.ShapeDtypeStruct((2, t, w), BF16), jax.ShapeDtypeStruct((t, w), BF16),
                   jax.ShapeDtypeStruct((t, 2 * w), BF16)],
        compiler_params=_cparams(("parallel",)),
        name="in_proj",
    )(*args)


def _flash_body(qi_ref, ki_ref, fast_ref, bound_ref, q_ref, k_ref, vx_ref, lam_ref, sw_ref, o_ref,
                m_ref, acc_ref, *, lam_init, chunk):
    p_idx = pl.program_id(1)
    qi = qi_ref[p_idx]
    ki = ki_ref[p_idx]
    tq = q_ref.shape[1]
    tk = k_ref.shape[0]
    fast = fast_ref[0] == 1
    diag = ki == qi

    @pl.when(ki == 0)
    def _():
        m_ref[...] = jnp.full(m_ref.shape, -jnp.inf, F32)
        acc_ref[...] = jnp.zeros(acc_ref.shape, F32)

    def keep_mask(c0, width):
        row = lax.broadcasted_iota(jnp.int32, (2 * tq, width), 0)
        col = lax.broadcasted_iota(jnp.int32, (2 * tq, width), 1) + c0
        return col <= jnp.where(row >= tq, row - tq, row)

    def head_operands(h):
        qk_lanes = pl.ds(pl.multiple_of(h * HEAD_W, HEAD_W), HEAD_W)
        v_lanes = pl.ds(pl.multiple_of(h * 2 * HEAD_W, 2 * HEAD_W), 2 * HEAD_W)
        return q_ref[:, :, qk_lanes].reshape(2 * tq, HEAD_W), qk_lanes, v_lanes

    def update_bounded(h, masked):
        q2, qk_lanes, v_lanes = head_operands(h)
        bound = bound_ref[0]
        for c0 in range(0, tk, chunk):
            k_c = k_ref[c0:c0 + chunk, qk_lanes]
            v_c = vx_ref[c0:c0 + chunk, v_lanes]
            if not masked:
                s = lax.dot_general(q2, k_c, (((1,), (1,)), ((), ())), preferred_element_type=F32)
                acc_ref[h] += jnp.dot(jnp.exp2(s - bound).astype(BF16), v_c, preferred_element_type=F32)
                continue
            rows = tq - c0
            q_low = q_ref[:, c0:, qk_lanes].reshape(2 * rows, HEAD_W)
            s = lax.dot_general(q_low, k_c, (((1,), (1,)), ((), ())), preferred_element_type=F32)
            row = lax.broadcasted_iota(jnp.int32, (2 * rows, chunk), 0)
            col = lax.broadcasted_iota(jnp.int32, (2 * rows, chunk), 1)
            keep = col <= jnp.where(row >= rows, row - rows, row)
            p = jnp.where(keep, jnp.exp2(s - bound), 0.0).astype(BF16)
            part = jnp.dot(p, v_c, preferred_element_type=F32)
            acc_ref[h, c0:tq] += part[:rows]
            acc_ref[h, tq + c0:] += part[rows:]

    def update_running_max(h, masked):
        q2, qk_lanes, v_lanes = head_operands(h)
        s = lax.dot_general(q2, k_ref[:, qk_lanes], (((1,), (1,)), ((), ())), preferred_element_type=F32)
        if masked:
            s = jnp.where(keep_mask(0, tk), s, -jnp.inf)
        m_prev = m_ref[h]
        m_new = jnp.maximum(m_prev, jnp.max(s, axis=-1, keepdims=True))
        p = jnp.exp2(s - m_new)
        acc_ref[h] = (jnp.exp2(m_prev - m_new) * acc_ref[h]
                      + jnp.dot(p.astype(BF16), vx_ref[:, v_lanes], preferred_element_type=F32))
        m_ref[h] = m_new

    def all_heads(update, masked):
        def body(h, carry):
            update(h, masked)
            return carry
        lax.fori_loop(0, ATT_HEADS, body, 0)

    for use_fast, update in ((True, update_bounded), (False, update_running_max)):
        path = fast if use_fast else jnp.logical_not(fast)

        @pl.when(jnp.logical_and(path, jnp.logical_not(diag)))
        def _(update=update):
            all_heads(update, False)

        @pl.when(jnp.logical_and(path, diag))
        def _(update=update):
            all_heads(update, True)

    @pl.when(diag)
    def _():
        lv = lam_ref[...]
        lam = (jnp.exp(jnp.sum(lv[0:1] * lv[1:2], axis=-1, keepdims=True))
               - jnp.exp(jnp.sum(lv[2:3] * lv[3:4], axis=-1, keepdims=True)) + lam_init)
        for h in range(ATT_HEADS):
            o = (acc_ref[h, 0:tq, 0:HEAD_W] / acc_ref[h, 0:tq, HEAD_W:]
                 - lam * (acc_ref[h, tq:, 0:HEAD_W] / acc_ref[h, tq:, HEAD_W:]))
            o_ref[:, h * HEAD_W:(h + 1) * HEAD_W] = (_rms(o, sw_ref[...]) * (1.0 - lam_init)).astype(o_ref.dtype)


def flash_diff_attention(qs, k, vx, q_norm_w, k_norm_w, lam_vec, subln_w, lam_init, batch, seq,
                         blk=ATT_BLOCK, chunk=ATT_CHUNK):
    t = k.shape[0]
    nq = seq // blk
    w = ATT_HEADS * HEAD_W
    pairs = [(i, j) for i in range(nq) for j in range(i + 1)]
    qi_tab = jnp.asarray([p[0] for p in pairs], jnp.int32)
    ki_tab = jnp.asarray([p[1] for p in pairs], jnp.int32)
    bound = (ATT_DIM ** 0.5) * jnp.max(jnp.abs(q_norm_w.astype(F32))) * jnp.max(jnp.abs(k_norm_w.astype(F32)))
    bound = bound * (LOG2E * BOUND_SLACK)
    fast = (bound <= MAX_SAFE_LOG2_BOUND).astype(jnp.int32).reshape(1)
    idx = lambda f: (lambda b, p, qi, ki, fast: f(b, qi[p], ki[p]))
    grid_spec = pltpu.PrefetchScalarGridSpec(
        num_scalar_prefetch=3,
        grid=(batch, len(pairs)),
        in_specs=[pl.BlockSpec(memory_space=pltpu.SMEM),
                  pl.BlockSpec((2, blk, w), idx(lambda b, qi, ki: (0, b * nq + qi, 0))),
                  pl.BlockSpec((blk, w), idx(lambda b, qi, ki: (b * nq + ki, 0))),
                  pl.BlockSpec((blk, 2 * w), idx(lambda b, qi, ki: (b * nq + ki, 0))),
                  pl.BlockSpec((4, ATT_DIM), idx(lambda b, qi, ki: (0, 0))),
                  pl.BlockSpec((1, HEAD_W), idx(lambda b, qi, ki: (0, 0)))],
        out_specs=pl.BlockSpec((blk, w), idx(lambda b, qi, ki: (b * nq + qi, 0))),
        scratch_shapes=[pltpu.VMEM((ATT_HEADS, 2 * blk, 1), F32),
                        pltpu.VMEM((ATT_HEADS, 2 * blk, 2 * HEAD_W), F32)],
    )
    return pl.pallas_call(
        functools.partial(_flash_body, lam_init=lam_init, chunk=chunk),
        grid_spec=grid_spec,
        out_shape=jax.ShapeDtypeStruct((t, w), BF16),
        compiler_params=_cparams(("parallel", "arbitrary")),
        name="flash_diff_attention",
    )(qi_tab, ki_tab, fast, bound.reshape(1), qs, k, vx, lam_vec.astype(F32),
      subln_w.astype(F32).reshape(1, HEAD_W))


def _softplus(x):
    return jnp.maximum(x, 0.0) + jnp.log1p(jnp.exp(-jnp.abs(x)))


def _conv_silu(x, halo, w):
    head = x[0:8]
    row8 = lax.broadcasted_iota(jnp.int32, head.shape, 0)
    y = x * w[CONV_K - 1:CONV_K]
    y_head = head * w[CONV_K - 1:CONV_K]
    for j in range(1, CONV_K):
        wj = w[CONV_K - 1 - j:CONV_K - j]
        y = y + pltpu.roll(x, j, 0) * wj
        shifted = jnp.where(row8 < j, pltpu.roll(halo, j, 0), pltpu.roll(head, j, 0))
        y_head = y_head + shifted * wj
    y = jnp.concatenate([y_head, y[8:]], axis=0)
    return y * jax.nn.sigmoid(y)


def _l2n(x):
    return x * lax.rsqrt(jnp.sum(x * x, axis=-1, keepdims=True) + NORM_EPS)


def _gdn_body(q_ref, k_ref, v_ref, hq_ref, hk_ref, hv_ref, sm_ref, cq_ref, ck_ref, cv_ref, al_ref, dt_ref,
              z_ref, nw_ref, o_ref, state_ref):
    n_streams, c = q_ref.shape[0], q_ref.shape[1]
    heads = range(GDN_HEADS)
    items = [(s, h) for s in range(n_streams) for h in heads]
    seq_start = pl.program_id(1) == 0

    @pl.when(seq_start)
    def _():
        state_ref[...] = jnp.zeros(state_ref.shape, F32)

    def conv(x_ref, halo_ref, w_ref, s):
        return _conv_silu(x_ref[s], jnp.where(seq_start, 0.0, halo_ref[s]), w_ref[...])

    row = lax.broadcasted_iota(jnp.int32, (c, c), 0)
    col = lax.broadcasted_iota(jnp.int32, (c, c), 1)
    incl = row >= col
    strict = row > col
    xor_idx = row ^ col
    lower_ones = jnp.where(incl, 1.0, 0.0).astype(F32)
    upper_ones = jnp.where(row <= col, 1.0, 0.0).astype(F32)
    sl = [slice(h * HEAD_W, (h + 1) * HEAD_W) for h in heads]

    q, k, v_beta, beta, gcol, grow = {}, {}, {}, {}, {}, {}
    for s in range(n_streams):
        q_all = conv(q_ref, hq_ref, cq_ref, s)
        k_all = conv(k_ref, hk_ref, ck_ref, s)
        v_all = conv(v_ref, hv_ref, cv_ref, s)
        raw = sm_ref[s]
        lane = lax.broadcasted_iota(jnp.int32, raw.shape, 1)
        sm = jnp.where(lane < GDN_HEADS, jax.nn.sigmoid(raw),
                       jnp.where(lane < 2 * GDN_HEADS, -jnp.exp(al_ref[...]) * _softplus(raw + dt_ref[...]), 0.0))
        gc_col = _hdot(lower_ones, sm)
        gc_row = _hdot(sm.T[0:8], upper_ones)
        for h in heads:
            q[s, h] = _l2n(q_all[:, sl[h]]) * (GDN_DK ** -0.5)
            k[s, h] = _l2n(k_all[:, sl[h]])
            beta[s, h] = sm[:, h:h + 1]
            v_beta[s, h] = v_all[:, sl[h]] * beta[s, h]
            gcol[s, h] = gc_col[:, GDN_HEADS + h:GDN_HEADS + h + 1]
            grow[s, h] = gc_row[GDN_HEADS + h:GDN_HEADS + h + 1, :]

    decay = {i: jnp.where(incl, jnp.exp(jnp.minimum(gcol[i] - grow[i], 0.0)), 0.0) for i in items}
    kb = {i: k[i] * beta[i] for i in items}
    l_mat = {i: jnp.where(strict, _bdot_nt(kb[i], k[i]) * decay[i], 0.0) for i in items}
    a_qk = {i: _bdot_nt(q[i], k[i]) * decay[i] for i in items}
    eg = {i: jnp.exp(gcol[i]) for i in items}
    rhs = {i: jnp.concatenate([v_beta[i], kb[i] * eg[i]], axis=1) for i in items}
    eye = jnp.where(row == col, 1.0, 0.0).astype(F32)
    x = {i: eye for i in items}
    b = 1
    while b < c:
        level = (xor_idx >= b) & (xor_idx < 2 * b)
        ex = {i: _bdot(jnp.where(level, l_mat[i], 0.0), x[i]) for i in items}
        x = {i: x[i] - _bdot(x[i], ex[i]) for i in items}
        b *= 2
    sol = {i: _bdot(x[i], rhs[i]) for i in items}
    state = {(s, h): state_ref[s * GDN_HEADS + h] for s, h in items}
    v_new = {i: sol[i][:, :HEAD_W] - _bdot(sol[i][:, HEAD_W:], state[i]) for i in items}
    o = {i: _bdot(q[i] * eg[i], state[i]) + _bdot(a_qk[i], v_new[i]) for i in items}
    nw = nw_ref[...]
    for s, h in items:
        i = (s, h)
        g_last = gcol[i][c - 1:c, :]
        k_dec = k[i] * jnp.exp(g_last - gcol[i])
        state_ref[s * GDN_HEADS + h] = state[i] * jnp.exp(g_last) + _bdot_tn(k_dec, v_new[i])
        z = z_ref[s, :, sl[h]]
        o_ref[s, :, sl[h]] = (_rms(o[i], nw) * (z * jax.nn.sigmoid(z))).astype(o_ref.dtype)


def gated_delta_net(proj, small, conv_w, a_log, dt_bias, gdn_norm_w, batch, seq, chunk=GDN_CHUNK):
    t = proj.shape[0]
    nc = seq // chunk
    w = GROUP_W
    n_streams = GDN_STREAMS if batch % GDN_STREAMS == 0 else 1
    groups = batch // n_streams
    proj3 = proj.reshape(n_streams, t // n_streams, proj.shape[1])
    small3 = small.reshape(n_streams, t // n_streams, HEAD_W)
    cw = conv_w.astype(F32)
    pad = lambda vec: jnp.zeros((1, HEAD_W), F32).at[0, GDN_HEADS:2 * GDN_HEADS].set(vec.astype(F32))
    col = lambda g: pl.BlockSpec((n_streams, chunk, w), lambda b, c, g=g: (0, b * nc + c, g))
    halo = lambda g: pl.BlockSpec((n_streams, 8, w),
                                  lambda b, c, g=g: (0, jnp.maximum((b * nc + c) * (chunk // 8) - 1, 0), g))
    const = lambda shape: pl.BlockSpec(shape, lambda b, c: (0, 0))
    out = pl.pallas_call(
        _gdn_body,
        grid=(groups, nc),
        in_specs=[col(COL_BQ), col(COL_BK), col(COL_BV), halo(COL_BQ), halo(COL_BK), halo(COL_BV),
                  pl.BlockSpec((n_streams, chunk, HEAD_W), lambda b, c: (0, b * nc + c, 0)),
                  const((CONV_K, w)), const((CONV_K, w)), const((CONV_K, w)),
                  const((1, HEAD_W)), const((1, HEAD_W)), col(COL_BZ), const((1, HEAD_W))],
        out_specs=pl.BlockSpec((n_streams, chunk, w), lambda b, c: (0, b * nc + c, 0)),
        out_shape=jax.ShapeDtypeStruct((n_streams, t // n_streams, w), BF16),
        scratch_shapes=[pltpu.VMEM((n_streams * GDN_HEADS, GDN_DK, HEAD_W), F32)],
        compiler_params=_cparams(("parallel", "arbitrary")),
        name="gated_delta_net",
    )(proj3, proj3, proj3, proj3, proj3, proj3, small3, cw[:, 0:w], cw[:, w:2 * w], cw[:, 2 * w:3 * w],
      pad(a_log), pad(dt_bias), proj3, gdn_norm_w.astype(F32).reshape(1, HEAD_W))
    return out.reshape(t, w)


def _top2_route(logits):
    lane = lax.broadcasted_iota(jnp.int32, logits.shape, 1).astype(F32)
    big = float(logits.shape[1])
    m1 = jnp.max(logits, axis=-1, keepdims=True)
    i1 = jnp.min(jnp.where(logits == m1, lane, big), axis=-1, keepdims=True)
    rest = jnp.where(lane == i1, -jnp.inf, logits)
    m2 = jnp.max(rest, axis=-1, keepdims=True)
    i2 = jnp.min(jnp.where(rest == m2, lane, big), axis=-1, keepdims=True)
    e = jnp.exp(m2 - m1)
    g1 = 1.0 / (1.0 + e)
    g2 = e / (1.0 + e)
    return jnp.where(lane == 0, i1, jnp.where(lane == 1, i2, jnp.where(lane == 2, g1, jnp.where(lane == 3, g2, 0.0))))


def _mixer_out_body(*refs, with_router):
    if with_router:
        (oa_ref, ob_ref, ga_ref, gb_ref, x_ref, wa_ref, wb_ref, wo_ref, nw_ref, rwh_ref, rwl_ref, rb_ref,
         xo_ref, ho_ref, ro_ref) = refs
    else:
        (oa_ref, ob_ref, ga_ref, gb_ref, x_ref, wa_ref, wb_ref, wo_ref, nw_ref, xo_ref, ho_ref) = refs
    ya = jnp.dot(oa_ref[...], wa_ref[...], preferred_element_type=F32)
    yb = jnp.dot(ob_ref[...], wb_ref[...], preferred_element_type=F32)
    merged = jax.nn.sigmoid(ga_ref[...]) * ya + jax.nn.sigmoid(gb_ref[...]) * yb
    x_new = x_ref[...] + jnp.dot(merged.astype(BF16), wo_ref[...], preferred_element_type=F32)
    xo_ref[...] = x_new
    hn = _rms(x_new, nw_ref[...])
    ho_ref[...] = hn.astype(ho_ref.dtype)
    if with_router:
        hn_hi = hn.astype(BF16)
        hn_lo = (hn - hn_hi.astype(F32)).astype(BF16)
        rw_hi = rwh_ref[...]
        logits = (jnp.dot(hn_hi, rw_hi, preferred_element_type=F32)
                  + jnp.dot(hn_lo, rw_hi, preferred_element_type=F32)
                  + jnp.dot(hn_hi, rwl_ref[...], preferred_element_type=F32))
        ro_ref[...] = _top2_route(logits + rb_ref[...])


def mixer_out(oa, ob, proj, x, w_a, w_b, w_o, norm_w, router=None, tm=512):
    t, d = x.shape
    w = GROUP_W
    row = lambda width, g=0: pl.BlockSpec((tm, width), lambda i, g=g: (i, g))
    const = lambda shape: pl.BlockSpec(shape, lambda i: (0, 0))
    in_specs = [row(w), row(w), row(d, COL_GA * w // d), row(d, COL_GB * w // d), row(d),
                const((w, d)), const((w, d)), const((d, d)), const((1, d))]
    args = [oa, ob, proj, proj, x, w_a, w_b, w_o, norm_w.astype(F32).reshape(1, d)]
    out_specs = [row(d), row(d)]
    out_shape = [jax.ShapeDtypeStruct((t, d), F32),
                 jax.ShapeDtypeStruct((t, d), F32 if router is not None else BF16)]
    if router is not None:
        rw, rb = router
        rw_pad = jnp.zeros((d, HEAD_W), F32).at[:, :N_EXPERTS].set(rw.astype(F32))
        rw_hi = rw_pad.astype(BF16)
        rw_lo = (rw_pad - rw_hi.astype(F32)).astype(BF16)
        rb_pad = jnp.full((1, HEAD_W), -jnp.inf, F32).at[0, :N_EXPERTS].set(rb.astype(F32))
        in_specs += [const((d, HEAD_W)), const((d, HEAD_W)), const((1, HEAD_W))]
        args += [rw_hi, rw_lo, rb_pad]
        out_specs.append(row(HEAD_W))
        out_shape.append(jax.ShapeDtypeStruct((t, HEAD_W), F32))
    return pl.pallas_call(
        functools.partial(_mixer_out_body, with_router=router is not None),
        grid=(t // tm,),
        in_specs=in_specs,
        out_specs=out_specs,
        out_shape=out_shape,
        compiler_params=_cparams(("parallel",)),
        name="mixer_out",
    )(*args)


def _ffn_body(*refs, with_norm, tc):
    if with_norm:
        h_ref, x_ref, wgu_ref, wd_ref, nw_ref, xo_ref, ho_ref, acc_ref = refs
    else:
        h_ref, x_ref, wgu_ref, wd_ref, xo_ref, acc_ref = refs
    h = h_ref[...]
    d_ff = wd_ref.shape[0]
    for ci, c0 in enumerate(range(0, d_ff, tc)):
        g = jnp.dot(h, wgu_ref[:, c0:c0 + tc], preferred_element_type=F32)
        u = jnp.dot(h, wgu_ref[:, d_ff + c0:d_ff + c0 + tc], preferred_element_type=F32)
        a = (g * jax.nn.sigmoid(g) * u).astype(BF16)
        part = jnp.dot(a, wd_ref[c0:c0 + tc, :], preferred_element_type=F32)
        if ci == 0:
            acc_ref[...] = x_ref[...] + part
        else:
            acc_ref[...] += part
    x_new = acc_ref[...]
    xo_ref[...] = x_new
    if with_norm:
        ho_ref[...] = _rms(x_new, nw_ref[...]).astype(ho_ref.dtype)


def dense_ffn(h, x, w_gu, w_d, next_norm_w=None, tm=512, tc=256):
    t, d = x.shape
    d_ff = w_d.shape[0]
    row = pl.BlockSpec((tm, d), lambda i: (i, 0))
    const = lambda shape: pl.BlockSpec(shape, lambda i: (0, 0), pipeline_mode=pl.Buffered(1))
    in_specs = [row, row, const((d, 2 * d_ff)), const((d_ff, d))]
    args = [h, x, w_gu, w_d]
    out_specs = [row]
    out_shape = [jax.ShapeDtypeStruct((t, d), F32)]
    if next_norm_w is not None:
        in_specs.append(pl.BlockSpec((1, d), lambda i: (0, 0)))
        args.append(next_norm_w.astype(F32).reshape(1, d))
        out_specs.append(row)
        out_shape.append(jax.ShapeDtypeStruct((t, d), BF16))
    return pl.pallas_call(
        functools.partial(_ffn_body, with_norm=next_norm_w is not None, tc=tc),
        grid=(t // tm,),
        in_specs=in_specs,
        out_specs=out_specs,
        out_shape=out_shape,
        scratch_shapes=[pltpu.VMEM((tm, d), F32)],
        compiler_params=_cparams(("parallel",)),
        name="dense_ffn",
    )(*args)


def _expert_body(be_ref, tok_ref, nl_ref, h_ref, wg_ref, wu_ref, wd_ref, o_ref, xbuf_ref, xb_ref, acc_ref, sem,
                 *, n_c):
    del be_ref
    blk = pl.program_id(0)
    c = pl.program_id(1)
    n_blk = pl.num_programs(0)
    tm = xb_ref.shape[0]
    slot = blk % 2
    live = blk < nl_ref[0]
    per_step = tm // n_c

    def row_copy(b, grp, i, s):
        tok = tok_ref[b * tm + grp * SUBLANES + i]
        return pltpu.make_async_copy(h_ref.at[pl.ds(tok, 1)], xbuf_ref.at[s, grp, pl.ds(i, 1)], sem.at[s])

    def wait_block(s):
        def body(g, carry):
            for j in range(WAIT_RUN // SUBLANES):
                for i in range(SUBLANES):
                    dst = xbuf_ref.at[s, g * (WAIT_RUN // SUBLANES) + j, pl.ds(i, 1)]
                    pltpu.make_async_copy(h_ref.at[pl.ds(0, 1)], dst, sem.at[s]).wait()
            return carry
        lax.fori_loop(0, tm // WAIT_RUN, body, 0)

    @pl.when(jnp.logical_and(blk == 0, c == 0))
    def _():
        def body(grp, carry):
            for i in range(SUBLANES):
                row_copy(0, grp, i, 0).start()
            return carry
        lax.fori_loop(0, tm // SUBLANES, body, 0)

    @pl.when(c == 0)
    def _():
        wait_block(slot)
        xb_ref[...] = xbuf_ref[slot].reshape(tm, xb_ref.shape[1]).astype(BF16)
        acc_ref[...] = jnp.zeros(acc_ref.shape, F32)

    nxt = jnp.where(blk + 1 < n_blk, blk + 1, 0)

    def gather_next():
        for j in range(per_step // SUBLANES):
            for i in range(SUBLANES):
                row_copy(nxt, c * (per_step // SUBLANES) + j, i, 1 - slot).start()

    @pl.when(live)
    def _():
        gather_next()
        xb = xb_ref[...]
        g = jnp.dot(xb, wg_ref[0], preferred_element_type=F32)
        u = jnp.dot(xb, wu_ref[0], preferred_element_type=F32)
        a = (g * jax.nn.sigmoid(g) * u).astype(BF16)
        acc_ref[...] += jnp.dot(a, wd_ref[0], preferred_element_type=F32)

    @pl.when(jnp.logical_not(live))
    def _():
        gather_next()

    @pl.when(c == n_c - 1)
    def _():
        o_ref[...] = acc_ref[...]

    @pl.when(jnp.logical_and(blk == n_blk - 1, c == n_c - 1))
    def _():
        wait_block(1 - slot)


def moe_experts(h, slot_tok, block_e, n_live, w_gu, w_d, tm=MOE_TM, tc=MOE_TC):
    n_slots = slot_tok.shape[0]
    d = h.shape[1]
    d_e = w_d.shape[1]
    n_c = d_e // tc
    assert tm % (n_c * SUBLANES) == 0, "each grid step gathers an equal share of the next block's row groups"
    assert tm % WAIT_RUN == 0 and WAIT_RUN % SUBLANES == 0
    grid_spec = pltpu.PrefetchScalarGridSpec(
        num_scalar_prefetch=3,
        grid=(n_slots // tm, n_c),
        in_specs=[pl.BlockSpec(memory_space=pl.ANY),
                  pl.BlockSpec((1, d, tc), lambda b, c, be, tok, nl: (be[b], 0, c)),
                  pl.BlockSpec((1, d, tc), lambda b, c, be, tok, nl: (be[b], 0, n_c + c)),
                  pl.BlockSpec((1, tc, d), lambda b, c, be, tok, nl: (be[b], c, 0))],
        out_specs=pl.BlockSpec((tm, d), lambda b, c, be, tok, nl: (b, 0)),
        scratch_shapes=[pltpu.VMEM((2, tm // SUBLANES, SUBLANES, d), F32), pltpu.VMEM((tm, d), BF16),
                        pltpu.VMEM((tm, d), F32), pltpu.SemaphoreType.DMA((2,))],
    )
    return pl.pallas_call(
        functools.partial(_expert_body, n_c=n_c),
        grid_spec=grid_spec,
        out_shape=jax.ShapeDtypeStruct((n_slots, d), F32),
        compiler_params=_cparams(("arbitrary", "arbitrary")),
        name="moe_experts",
    )(block_e, slot_tok, n_live, h, w_gu, w_gu, w_d)


def _combine_body(dest_ref, x_ref, r_ref, yb_ref, o_ref, buf_ref, sem, *, tb):
    step = pl.program_id(0)
    slot = step % 2

    def issue(st, s):
        def body(g, carry):
            for i in range(SUBLANES):
                for k in range(2):
                    src = yb_ref.at[pl.ds(dest_ref[2 * (st * tb + g * SUBLANES + i) + k], 1)]
                    pltpu.make_async_copy(src, buf_ref.at[s, k, g, pl.ds(i, 1)], sem.at[s]).start()
            return carry
        lax.fori_loop(0, tb // SUBLANES, body, 0)

    def wait_all(s):
        def body(g, carry):
            for j in range(WAIT_RUN // SUBLANES):
                for i in range(SUBLANES):
                    for k in range(2):
                        dst = buf_ref.at[s, k, g * (WAIT_RUN // SUBLANES) + j, pl.ds(i, 1)]
                        pltpu.make_async_copy(yb_ref.at[pl.ds(0, 1)], dst, sem.at[s]).wait()
            return carry
        lax.fori_loop(0, tb // WAIT_RUN, body, 0)

    @pl.when(step == 0)
    def _():
        issue(0, 0)

    @pl.when(step + 1 < pl.num_programs(0))
    def _():
        issue(step + 1, 1 - slot)

    wait_all(slot)
    route = r_ref[...]
    y0 = buf_ref[slot, 0].reshape(o_ref.shape)
    y1 = buf_ref[slot, 1].reshape(o_ref.shape)
    o_ref[...] = x_ref[...] + route[:, 2:3] * y0 + route[:, 3:4] * y1


def moe_combine(x, route, yb, dest, tb=256):
    t, d = x.shape
    assert tb % WAIT_RUN == 0
    grid_spec = pltpu.PrefetchScalarGridSpec(
        num_scalar_prefetch=1,
        grid=(t // tb,),
        in_specs=[pl.BlockSpec((tb, d), lambda i, dest: (i, 0)),
                  pl.BlockSpec((tb, HEAD_W), lambda i, dest: (i, 0)),
                  pl.BlockSpec(memory_space=pl.ANY)],
        out_specs=pl.BlockSpec((tb, d), lambda i, dest: (i, 0)),
        scratch_shapes=[pltpu.VMEM((2, 2, tb // SUBLANES, SUBLANES, d), F32), pltpu.SemaphoreType.DMA((2,))],
    )
    return pl.pallas_call(
        functools.partial(_combine_body, tb=tb),
        grid_spec=grid_spec,
        out_shape=jax.ShapeDtypeStruct((t, d), F32),
        compiler_params=_cparams(("arbitrary",)),
        name="moe_combine",
    )(dest, x, route, yb)


def moe_layer(hn, x, route, w_gu, w_d, tm=MOE_TM):
    t, d = x.shape
    n_assign = 2 * t
    e_flat = route[:, :2].astype(jnp.int32).reshape(n_assign)
    onehot = (e_flat[:, None] == jnp.arange(N_EXPERTS, dtype=jnp.int32)[None, :]).astype(jnp.int32)
    incl = jnp.cumsum(onehot, axis=0)
    rank = jnp.sum((incl - onehot) * onehot, axis=1)
    counts = incl[-1]
    padded = (counts + tm - 1) // tm * tm
    pad_end = jnp.cumsum(padded)
    pad_start = pad_end - padded
    dest = (pad_start[e_flat] + rank).astype(jnp.int32)
    n_blocks = -(-n_assign // tm) + N_EXPERTS
    block_start = jnp.arange(n_blocks, dtype=jnp.int32) * tm
    block_e = jnp.minimum(jnp.sum((pad_end[None, :] <= block_start[:, None]).astype(jnp.int32), axis=1),
                          N_EXPERTS - 1).astype(jnp.int32)
    slot_tok = jnp.zeros((n_blocks * tm,), jnp.int32).at[dest].set(jnp.arange(n_assign, dtype=jnp.int32) // 2,
                                                                  unique_indices=True)
    n_live = (pad_end[-1:] // tm).astype(jnp.int32)
    yb = moe_experts(hn, slot_tok, block_e, n_live, w_gu, w_d)
    return moe_combine(x, route, yb, dest)


def _rope_coefficients(positions):
    half = ROT_DIM // 2
    lane = jnp.arange(HEAD_W) % ATT_DIM
    inv_freq = ROPE_THETA ** (-(2.0 * (lane % half)).astype(F32) / ROT_DIM)
    freq = jnp.where(lane < ROT_DIM, inv_freq, 0.0)
    ang = positions.astype(F32).reshape(-1, 1) * freq[None, :]
    sin = jnp.sin(ang)
    return (jnp.cos(ang), jnp.where(lane < half, -sin, 0.0),
            jnp.where((lane >= half) & (lane < ROT_DIM), sin, 0.0))


def _split_w_in(w_in):
    att = 3 * GROUP_W
    conv = 3 * GROUP_W
    o_z = att + conv
    o_small = o_z + GROUP_W
    o_ga = o_small + 2 * GDN_HEADS
    d = w_in.shape[0]
    main = jnp.concatenate([w_in[:, o_ga:], w_in[:, :o_small]], axis=1).astype(BF16)
    small = jnp.zeros((d, HEAD_W), F32).at[:, :2 * GDN_HEADS].set(w_in[:, o_small:o_ga]).astype(BF16)
    return main, small


def kernel(x, positions, norm_mix_w, w_in, q_norm_w, k_norm_w, lam_vec, subln_w, conv_w, a_log, dt_bias,
           gdn_norm_w, w_branch_a, w_branch_b, w_out, norm_ffn_w, ffn_w_gate_up, ffn_w_down, router_w,
           router_b, moe_w_gate_up, moe_w_down):
    batch, seq, d = x.shape
    depth = w_in.shape[0]
    t = batch * seq
    xt = x.reshape(t, d).astype(F32)
    rope_c, rope_s1, rope_s2 = _rope_coefficients(positions)
    h = None
    for layer in range(depth):
        lam_init = 0.8 - 0.6 * math.exp(-0.3 * layer)
        is_moe = layer % 2 == 1
        j = layer // 2
        w_main, w_small = _split_w_in(w_in[layer])
        rope = (rope_c, rope_s1, rope_s2)
        if h is None:
            proj, small, qs, kk, vx = in_proj(xt, w_main, w_small, rope, q_norm_w[layer], k_norm_w[layer],
                                              norm_mix_w[layer])
        else:
            proj, small, qs, kk, vx = in_proj(h, w_main, w_small, rope, q_norm_w[layer], k_norm_w[layer])
        oa = flash_diff_attention(qs, kk, vx, q_norm_w[layer], k_norm_w[layer], lam_vec[layer], subln_w[layer],
                                  lam_init, batch, seq)
        ob = gated_delta_net(proj, small, conv_w[layer], a_log[layer], dt_bias[layer], gdn_norm_w[layer], batch, seq)
        router = (router_w[j], router_b[j]) if is_moe else None
        outs = mixer_out(oa, ob, proj, xt, w_branch_a[layer].astype(BF16), w_branch_b[layer].astype(BF16),
                         w_out[layer].astype(BF16), norm_ffn_w[layer], router)
        if is_moe:
            xt, hn, route = outs
            xt = moe_layer(hn, xt, route, moe_w_gate_up[j].astype(BF16), moe_w_down[j].astype(BF16))
            h = None
        else:
            xt, hn = outs
            nxt = norm_mix_w[layer + 1] if layer + 1 < depth else None
            res = dense_ffn(hn, xt, ffn_w_gate_up[j].astype(BF16), ffn_w_down[j].astype(BF16), nxt)
            if nxt is not None:
                xt, h = res
            else:
                xt = res[0]
    return xt.reshape(batch, seq, d)
```

```python
import functools
import math

import jax
import jax.numpy as jnp
from jax import lax
from jax.experimental import pallas as pl
from jax.experimental.pallas import tpu as pltpu

F32 = jnp.float32
BF16 = jnp.bfloat16

ATT_HEADS = 4
ATT_DIM = 64
ROT_DIM = ATT_DIM // 4
ROPE_THETA = 500000.0
GDN_HEADS = 4
GDN_DK = 128
CONV_K = 4
N_EXPERTS = 8
NORM_EPS = 1e-6

HEAD_W = 128
GROUP_W = 512
COL_GA, COL_GB, COL_AQ, COL_AK, COL_AV, COL_BQ, COL_BK, COL_BV, COL_BZ = 0, 2, 4, 5, 6, 7, 8, 9, 10
N_GROUPS = 11

VMEM_LIMIT = 56 * 1024 * 1024

GDN_CHUNK = 256
GDN_STREAMS = 2
ATT_BLOCK = 1024
ATT_CHUNK = 256
LOG2E = math.log2(math.e)
BOUND_SLACK = 1.01
MAX_SAFE_LOG2_BOUND = 56.0
MOE_TM = 896
MOE_TC = 512
WAIT_RUN = 16
SUBLANES = 8


def _cparams(sem):
    return pltpu.CompilerParams(dimension_semantics=sem, vmem_limit_bytes=VMEM_LIMIT)


def _bdot(a, b):
    return jnp.dot(a.astype(BF16), b.astype(BF16), preferred_element_type=F32)


def _bdot_nt(a, b):
    return lax.dot_general(a.astype(BF16), b.astype(BF16), (((1,), (1,)), ((), ())),
                           preferred_element_type=F32)


def _bdot_tn(a, b):
    return lax.dot_general(a.astype(BF16), b.astype(BF16), (((0,), (0,)), ((), ())),
                           preferred_element_type=F32)


def _hdot(a, b):
    return jnp.dot(a, b, preferred_element_type=F32, precision=lax.Precision.HIGHEST)


def _rms(x, w):
    return x * lax.rsqrt(jnp.mean(x * x, axis=-1, keepdims=True) + NORM_EPS) * w


def _group_mean_sq(x, m_ref):
    ss = x * x
    hi = ss.astype(BF16)
    lo = (ss - hi.astype(F32)).astype(BF16)
    m = m_ref[...]
    gs = jnp.dot(hi, m, preferred_element_type=F32) + jnp.dot(lo, m, preferred_element_type=F32)
    return gs * (1.0 / ATT_DIM)


def _in_proj_body(*refs, with_norm):
    if with_norm:
        x_ref, nw_ref = refs[:2]
        h = _rms(x_ref[...], nw_ref[...]).astype(BF16)
        refs = refs[2:]
    else:
        h = refs[0][...]
        refs = refs[1:]
    (w_ref, ws_ref, c_ref, s1_ref, s2_ref, qw_ref, kw_ref, m_ref,
     o_ref, os_ref, qs_ref, ko_ref, vx_ref) = refs
    w = GROUP_W
    lo, hi = COL_AQ * w, (COL_AV + 1) * w
    n = w_ref.shape[1]
    o_ref[:, 0:lo] = jnp.dot(h, w_ref[:, 0:lo], preferred_element_type=F32)
    qkv = jnp.dot(h, w_ref[:, lo:hi], preferred_element_type=F32)
    o_ref[:, lo:hi] = qkv
    o_ref[:, hi:n] = jnp.dot(h, w_ref[:, hi:n], preferred_element_type=F32)
    os_ref[...] = jnp.dot(h, ws_ref[...], preferred_element_type=F32)

    reps = w // HEAD_W
    c = jnp.concatenate([c_ref[...]] * reps, axis=1)
    s1 = jnp.concatenate([s1_ref[...]] * reps, axis=1)
    s2 = jnp.concatenate([s2_ref[...]] * reps, axis=1)

    def norm_rope(x, nw):
        y = x * lax.rsqrt(_group_mean_sq(x, m_ref) + NORM_EPS) * nw
        half = ROT_DIM // 2
        return y * c + pltpu.roll(y, w - half, 1) * s1 + pltpu.roll(y, half, 1) * s2

    q = norm_rope(qkv[:, 0:w], qw_ref[...]) * (ATT_DIM ** -0.5 * LOG2E)
    lane = lax.broadcasted_iota(jnp.int32, q.shape, 1)
    first_map = (lane % HEAD_W) < ATT_DIM
    qs_ref[0] = jnp.where(first_map, q, 0.0).astype(BF16)
    qs_ref[1] = jnp.where(first_map, 0.0, q).astype(BF16)
    ko_ref[...] = norm_rope(qkv[:, w:2 * w], kw_ref[...]).astype(BF16)
    v = qkv[:, 2 * w:3 * w].astype(BF16)
    ones = jnp.ones((v.shape[0], HEAD_W), BF16)
    parts = []
    for hh in range(reps):
        parts += [v[:, hh * HEAD_W:(hh + 1) * HEAD_W], ones]
    vx_ref[...] = jnp.concatenate(parts, axis=1)


def in_proj(h, w_main, w_small, rope, q_norm_w, k_norm_w, norm_w=None, tm=512):
    t, d = h.shape
    n = w_main.shape[1]
    w = GROUP_W
    grp = jnp.arange(w) // ATT_DIM
    ones_bd = (grp[:, None] == grp[None, :]).astype(BF16)
    qw = jnp.tile(q_norm_w.astype(F32), w // ATT_DIM).reshape(1, w)
    kw = jnp.tile(k_norm_w.astype(F32), w // ATT_DIM).reshape(1, w)
    const = lambda shape: pl.BlockSpec(shape, lambda i: (0, 0), pipeline_mode=pl.Buffered(1))
    small_const = lambda shape: pl.BlockSpec(shape, lambda i: (0, 0))
    tab = pl.BlockSpec((tm, HEAD_W), lambda i: (i, 0))
    in_specs = [pl.BlockSpec((tm, d), lambda i: (i, 0))]
    args = [h]
    if norm_w is not None:
        in_specs.append(small_const((1, d)))
        args.append(norm_w.astype(F32).reshape(1, d))
    in_specs += [const((d, n)), const((d, HEAD_W)), tab, tab, tab,
                 small_const((1, w)), small_const((1, w)), small_const((w, w))]
    args += [w_main, w_small, *rope, qw, kw, ones_bd]
    return pl.pallas_call(
        functools.partial(_in_proj_body, with_norm=norm_w is not None),
        grid=(t // tm,),
        in_specs=in_specs,
        out_specs=[pl.BlockSpec((tm, n), lambda i: (i, 0)), pl.BlockSpec((tm, HEAD_W), lambda i: (i, 0)),
                   pl.BlockSpec((2, tm, w), lambda i: (0, i, 0)), pl.BlockSpec((tm, w), lambda i: (i, 0)),
                   pl.BlockSpec((tm, 2 * w), lambda i: (i, 0))],
        out_shape=[jax.ShapeDtypeStruct((t, n), F32), jax.ShapeDtypeStruct((t, HEAD_W), F32),
                   jax.ShapeDtypeStruct((2, t, w), BF16), jax.ShapeDtypeStruct((t, w), BF16),
                   jax.ShapeDtypeStruct((t, 2 * w), BF16)],
        compiler_params=_cparams(("parallel",)),
        name="in_proj",
    )(*args)


def _flash_body(qi_ref, ki_ref, fast_ref, bound_ref, q_ref, k_ref, vx_ref, lam_ref, sw_ref, o_ref,
                m_ref, acc_ref, *, lam_init, chunk):
    p_idx = pl.program_id(1)
    qi = qi_ref[p_idx]
    ki = ki_ref[p_idx]
    tq = q_ref.shape[1]
    tk = k_ref.shape[0]
    fast = fast_ref[0] == 1
    diag = ki == qi

    @pl.when(ki == 0)
    def _():
        m_ref[...] = jnp.full(m_ref.shape, -jnp.inf, F32)
        acc_ref[...] = jnp.zeros(acc_ref.shape, F32)

    def keep_mask(c0, width):
        row = lax.broadcasted_iota(jnp.int32, (2 * tq, width), 0)
        col = lax.broadcasted_iota(jnp.int32, (2 * tq, width), 1) + c0
        return col <= jnp.where(row >= tq, row - tq, row)

    def head_operands(h):
        qk_lanes = pl.ds(pl.multiple_of(h * HEAD_W, HEAD_W), HEAD_W)
        v_lanes = pl.ds(pl.multiple_of(h * 2 * HEAD_W, 2 * HEAD_W), 2 * HEAD_W)
        return q_ref[:, :, qk_lanes].reshape(2 * tq, HEAD_W), qk_lanes, v_lanes

    def update_bounded(h, masked):
        q2, qk_lanes, v_lanes = head_operands(h)
        bound = bound_ref[0]
        for c0 in range(0, tk, chunk):
            k_c = k_ref[c0:c0 + chunk, qk_lanes]
            v_c = vx_ref[c0:c0 + chunk, v_lanes]
            if not masked:
                s = lax.dot_general(q2, k_c, (((1,), (1,)), ((), ())), preferred_element_type=F32)
                acc_ref[h] += jnp.dot(jnp.exp2(s - bound).astype(BF16), v_c, preferred_element_type=F32)
                continue
            rows = tq - c0
            q_low = q_ref[:, c0:, qk_lanes].reshape(2 * rows, HEAD_W)
            s = lax.dot_general(q_low, k_c, (((1,), (1,)), ((), ())), preferred_element_type=F32)
            row = lax.broadcasted_iota(jnp.int32, (2 * rows, chunk), 0)
            col = lax.broadcasted_iota(jnp.int32, (2 * rows, chunk), 1)
            keep = col <= jnp.where(row >= rows, row - rows, row)
            p = jnp.where(keep, jnp.exp2(s - bound), 0.0).astype(BF16)
            part = jnp.dot(p, v_c, preferred_element_type=F32)
            acc_ref[h, c0:tq] += part[:rows]
            acc_ref[h, tq + c0:] += part[rows:]

    def update_running_max(h, masked):
        q2, qk_lanes, v_lanes = head_operands(h)
        s = lax.dot_general(q2, k_ref[:, qk_lanes], (((1,), (1,)), ((), ())), preferred_element_type=F32)
        if masked:
            s = jnp.where(keep_mask(0, tk), s, -jnp.inf)
        m_prev = m_ref[h]
        m_new = jnp.maximum(m_prev, jnp.max(s, axis=-1, keepdims=True))
        p = jnp.exp2(s - m_new)
        acc_ref[h] = (jnp.exp2(m_prev - m_new) * acc_ref[h]
                      + jnp.dot(p.astype(BF16), vx_ref[:, v_lanes], preferred_element_type=F32))
        m_ref[h] = m_new

    def all_heads(update, masked):
        def body(h, carry):
            update(h, masked)
            return carry
        lax.fori_loop(0, ATT_HEADS, body, 0)

    for use_fast, update in ((True, update_bounded), (False, update_running_max)):
        path = fast if use_fast else jnp.logical_not(fast)

        @pl.when(jnp.logical_and(path, jnp.logical_not(diag)))
        def _(update=update):
            all_heads(update, False)

        @pl.when(jnp.logical_and(path, diag))
        def _(update=update):
            all_heads(update, True)

    @pl.when(diag)
    def _():
        lv = lam_ref[...]
        lam = (jnp.exp(jnp.sum(lv[0:1] * lv[1:2], axis=-1, keepdims=True))
               - jnp.exp(jnp.sum(lv[2:3] * lv[3:4], axis=-1, keepdims=True)) + lam_init)
        for h in range(ATT_HEADS):
            o = (acc_ref[h, 0:tq, 0:HEAD_W] / acc_ref[h, 0:tq, HEAD_W:]
                 - lam * (acc_ref[h, tq:, 0:HEAD_W] / acc_ref[h, tq:, HEAD_W:]))
            o_ref[:, h * HEAD_W:(h + 1) * HEAD_W] = (_rms(o, sw_ref[...]) * (1.0 - lam_init)).astype(o_ref.dtype)


def flash_diff_attention(qs, k, vx, q_norm_w, k_norm_w, lam_vec, subln_w, lam_init, batch, seq,
                         blk=ATT_BLOCK, chunk=ATT_CHUNK):
    t = k.shape[0]
    nq = seq // blk
    w = ATT_HEADS * HEAD_W
    pairs = [(i, j) for i in range(nq) for j in range(i + 1)]
    qi_tab = jnp.asarray([p[0] for p in pairs], jnp.int32)
    ki_tab = jnp.asarray([p[1] for p in pairs], jnp.int32)
    bound = (ATT_DIM ** 0.5) * jnp.max(jnp.abs(q_norm_w.astype(F32))) * jnp.max(jnp.abs(k_norm_w.astype(F32)))
    bound = bound * (LOG2E * BOUND_SLACK)
    fast = (bound <= MAX_SAFE_LOG2_BOUND).astype(jnp.int32).reshape(1)
    idx = lambda f: (lambda b, p, qi, ki, fast: f(b, qi[p], ki[p]))
    grid_spec = pltpu.PrefetchScalarGridSpec(
        num_scalar_prefetch=3,
        grid=(batch, len(pairs)),
        in_specs=[pl.BlockSpec(memory_space=pltpu.SMEM),
                  pl.BlockSpec((2, blk, w), idx(lambda b, qi, ki: (0, b * nq + qi, 0))),
                  pl.BlockSpec((blk, w), idx(lambda b, qi, ki: (b * nq + ki, 0))),
                  pl.BlockSpec((blk, 2 * w), idx(lambda b, qi, ki: (b * nq + ki, 0))),
                  pl.BlockSpec((4, ATT_DIM), idx(lambda b, qi, ki: (0, 0))),
                  pl.BlockSpec((1, HEAD_W), idx(lambda b, qi, ki: (0, 0)))],
        out_specs=pl.BlockSpec((blk, w), idx(lambda b, qi, ki: (b * nq + qi, 0))),
        scratch_shapes=[pltpu.VMEM((ATT_HEADS, 2 * blk, 1), F32),
                        pltpu.VMEM((ATT_HEADS, 2 * blk, 2 * HEAD_W), F32)],
    )
    return pl.pallas_call(
        functools.partial(_flash_body, lam_init=lam_init, chunk=chunk),
        grid_spec=grid_spec,
        out_shape=jax.ShapeDtypeStruct((t, w), BF16),
        compiler_params=_cparams(("parallel", "arbitrary")),
        name="flash_diff_attention",
    )(qi_tab, ki_tab, fast, bound.reshape(1), qs, k, vx, lam_vec.astype(F32),
      subln_w.astype(F32).reshape(1, HEAD_W))


def _softplus(x):
    return jnp.maximum(x, 0.0) + jnp.log1p(jnp.exp(-jnp.abs(x)))


def _conv_silu(x, halo, w):
    head = x[0:8]
    row8 = lax.broadcasted_iota(jnp.int32, head.shape, 0)
    y = x * w[CONV_K - 1:CONV_K]
    y_head = head * w[CONV_K - 1:CONV_K]
    for j in range(1, CONV_K):
        wj = w[CONV_K - 1 - j:CONV_K - j]
        y = y + pltpu.roll(x, j, 0) * wj
        shifted = jnp.where(row8 < j, pltpu.roll(halo, j, 0), pltpu.roll(head, j, 0))
        y_head = y_head + shifted * wj
    y = jnp.concatenate([y_head, y[8:]], axis=0)
    return y * jax.nn.sigmoid(y)


def _l2n(x):
    return x * lax.rsqrt(jnp.sum(x * x, axis=-1, keepdims=True) + NORM_EPS)


def _gdn_body(q_ref, k_ref, v_ref, hq_ref, hk_ref, hv_ref, sm_ref, cq_ref, ck_ref, cv_ref, al_ref, dt_ref,
              z_ref, nw_ref, o_ref, state_ref):
    n_streams, c = q_ref.shape[0], q_ref.shape[1]
    heads = range(GDN_HEADS)
    items = [(s, h) for s in range(n_streams) for h in heads]
    seq_start = pl.program_id(1) == 0

    @pl.when(seq_start)
    def _():
        state_ref[...] = jnp.zeros(state_ref.shape, F32)

    def conv(x_ref, halo_ref, w_ref, s):
        return _conv_silu(x_ref[s], jnp.where(seq_start, 0.0, halo_ref[s]), w_ref[...])

    row = lax.broadcasted_iota(jnp.int32, (c, c), 0)
    col = lax.broadcasted_iota(jnp.int32, (c, c), 1)
    incl = row >= col
    strict = row > col
    xor_idx = row ^ col
    lower_ones = jnp.where(incl, 1.0, 0.0).astype(F32)
    upper_ones = jnp.where(row <= col, 1.0, 0.0).astype(F32)
    sl = [slice(h * HEAD_W, (h + 1) * HEAD_W) for h in heads]

    q, k, v_beta, beta, gcol, grow = {}, {}, {}, {}, {}, {}
    for s in range(n_streams):
        q_all = conv(q_ref, hq_ref, cq_ref, s)
        k_all = conv(k_ref, hk_ref, ck_ref, s)
        v_all = conv(v_ref, hv_ref, cv_ref, s)
        raw = sm_ref[s]
        lane = lax.broadcasted_iota(jnp.int32, raw.shape, 1)
        sm = jnp.where(lane < GDN_HEADS, jax.nn.sigmoid(raw),
                       jnp.where(lane < 2 * GDN_HEADS, -jnp.exp(al_ref[...]) * _softplus(raw + dt_ref[...]), 0.0))
        gc_col = _hdot(lower_ones, sm)
        gc_row = _hdot(sm.T[0:8], upper_ones)
        for h in heads:
            q[s, h] = _l2n(q_all[:, sl[h]]) * (GDN_DK ** -0.5)
            k[s, h] = _l2n(k_all[:, sl[h]])
            beta[s, h] = sm[:, h:h + 1]
            v_beta[s, h] = v_all[:, sl[h]] * beta[s, h]
            gcol[s, h] = gc_col[:, GDN_HEADS + h:GDN_HEADS + h + 1]
            grow[s, h] = gc_row[GDN_HEADS + h:GDN_HEADS + h + 1, :]

    decay = {i: jnp.where(incl, jnp.exp(jnp.minimum(gcol[i] - grow[i], 0.0)), 0.0) for i in items}
    kb = {i: k[i] * beta[i] for i in items}
    l_mat = {i: jnp.where(strict, _bdot_nt(kb[i], k[i]) * decay[i], 0.0) for i in items}
    a_qk = {i: _bdot_nt(q[i], k[i]) * decay[i] for i in items}
    eg = {i: jnp.exp(gcol[i]) for i in items}
    rhs = {i: jnp.concatenate([v_beta[i], kb[i] * eg[i]], axis=1) for i in items}
    eye = jnp.where(row == col, 1.0, 0.0).astype(F32)
    x = {i: eye for i in items}
    b = 1
    while b < c:
        level = (xor_idx >= b) & (xor_idx < 2 * b)
        ex = {i: _bdot(jnp.where(level, l_mat[i], 0.0), x[i]) for i in items}
        x = {i: x[i] - _bdot(x[i], ex[i]) for i in items}
        b *= 2
    sol = {i: _bdot(x[i], rhs[i]) for i in items}
    state = {(s, h): state_ref[s * GDN_HEADS + h] for s, h in items}
    v_new = {i: sol[i][:, :HEAD_W] - _bdot(sol[i][:, HEAD_W:], state[i]) for i in items}
    o = {i: _bdot(q[i] * eg[i], state[i]) + _bdot(a_qk[i], v_new[i]) for i in items}
    nw = nw_ref[...]
    for s, h in items:
        i = (s, h)
        g_last = gcol[i][c - 1:c, :]
        k_dec = k[i] * jnp.exp(g_last - gcol[i])
        state_ref[s * GDN_HEADS + h] = state[i] * jnp.exp(g_last) + _bdot_tn(k_dec, v_new[i])
        z = z_ref[s, :, sl[h]]
        o_ref[s, :, sl[h]] = (_rms(o[i], nw) * (z * jax.nn.sigmoid(z))).astype(o_ref.dtype)


def gated_delta_net(proj, small, conv_w, a_log, dt_bias, gdn_norm_w, batch, seq, chunk=GDN_CHUNK):
    t = proj.shape[0]
    nc = seq // chunk
    w = GROUP_W
    n_streams = GDN_STREAMS if batch % GDN_STREAMS == 0 else 1
    groups = batch // n_streams
    proj3 = proj.reshape(n_streams, t // n_streams, proj.shape[1])
    small3 = small.reshape(n_streams, t // n_streams, HEAD_W)
    cw = conv_w.astype(F32)
    pad = lambda vec: jnp.zeros((1, HEAD_W), F32).at[0, GDN_HEADS:2 * GDN_HEADS].set(vec.astype(F32))
    col = lambda g: pl.BlockSpec((n_streams, chunk, w), lambda b, c, g=g: (0, b * nc + c, g))
    halo = lambda g: pl.BlockSpec((n_streams, 8, w),
                                  lambda b, c, g=g: (0, jnp.maximum((b * nc + c) * (chunk // 8) - 1, 0), g))
    const = lambda shape: pl.BlockSpec(shape, lambda b, c: (0, 0))
    out = pl.pallas_call(
        _gdn_body,
        grid=(groups, nc),
        in_specs=[col(COL_BQ), col(COL_BK), col(COL_BV), halo(COL_BQ), halo(COL_BK), halo(COL_BV),
                  pl.BlockSpec((n_streams, chunk, HEAD_W), lambda b, c: (0, b * nc + c, 0)),
                  const((CONV_K, w)), const((CONV_K, w)), const((CONV_K, w)),
                  const((1, HEAD_W)), const((1, HEAD_W)), col(COL_BZ), const((1, HEAD_W))],
        out_specs=pl.BlockSpec((n_streams, chunk, w), lambda b, c: (0, b * nc + c, 0)),
        out_shape=jax.ShapeDtypeStruct((n_streams, t // n_streams, w), BF16),
        scratch_shapes=[pltpu.VMEM((n_streams * GDN_HEADS, GDN_DK, HEAD_W), F32)],
        compiler_params=_cparams(("parallel", "arbitrary")),
        name="gated_delta_net",
    )(proj3, proj3, proj3, proj3, proj3, proj3, small3, cw[:, 0:w], cw[:, w:2 * w], cw[:, 2 * w:3 * w],
      pad(a_log), pad(dt_bias), proj3, gdn_norm_w.astype(F32).reshape(1, HEAD_W))
    return out.reshape(t, w)


def _top2_route(logits):
    lane = lax.broadcasted_iota(jnp.int32, logits.shape, 1).astype(F32)
    big = float(logits.shape[1])
    m1 = jnp.max(logits, axis=-1, keepdims=True)
    i1 = jnp.min(jnp.where(logits == m1, lane, big), axis=-1, keepdims=True)
    rest = jnp.where(lane == i1, -jnp.inf, logits)
    m2 = jnp.max(rest, axis=-1, keepdims=True)
    i2 = jnp.min(jnp.where(rest == m2, lane, big), axis=-1, keepdims=True)
    e = jnp.exp(m2 - m1)
    g1 = 1.0 / (1.0 + e)
    g2 = e / (1.0 + e)
    return jnp.where(lane == 0, i1, jnp.where(lane == 1, i2, jnp.where(lane == 2, g1, jnp.where(lane == 3, g2, 0.0))))


def _mixer_out_body(*refs, with_router):
    if with_router:
        (oa_ref, ob_ref, ga_ref, gb_ref, x_ref, wa_ref, wb_ref, wo_ref, nw_ref, rwh_ref, rwl_ref, rb_ref,
         xo_ref, ho_ref, ro_ref) = refs
    else:
        (oa_ref, ob_ref, ga_ref, gb_ref, x_ref, wa_ref, wb_ref, wo_ref, nw_ref, xo_ref, ho_ref) = refs
    ya = jnp.dot(oa_ref[...], wa_ref[...], preferred_element_type=F32)
    yb = jnp.dot(ob_ref[...], wb_ref[...], preferred_element_type=F32)
    merged = jax.nn.sigmoid(ga_ref[...]) * ya + jax.nn.sigmoid(gb_ref[...]) * yb
    x_new = x_ref[...] + jnp.dot(merged.astype(BF16), wo_ref[...], preferred_element_type=F32)
    xo_ref[...] = x_new
    hn = _rms(x_new, nw_ref[...])
    ho_ref[...] = hn.astype(ho_ref.dtype)
    if with_router:
        hn_hi = hn.astype(BF16)
        hn_lo = (hn - hn_hi.astype(F32)).astype(BF16)
        rw_hi = rwh_ref[...]
        logits = (jnp.dot(hn_hi, rw_hi, preferred_element_type=F32)
                  + jnp.dot(hn_lo, rw_hi, preferred_element_type=F32)
                  + jnp.dot(hn_hi, rwl_ref[...], preferred_element_type=F32))
        ro_ref[...] = _top2_route(logits + rb_ref[...])


def mixer_out(oa, ob, proj, x, w_a, w_b, w_o, norm_w, router=None, tm=512):
    t, d = x.shape
    w = GROUP_W
    row = lambda width, g=0: pl.BlockSpec((tm, width), lambda i, g=g: (i, g))
    const = lambda shape: pl.BlockSpec(shape, lambda i: (0, 0))
    in_specs = [row(w), row(w), row(d, COL_GA * w // d), row(d, COL_GB * w // d), row(d),
                const((w, d)), const((w, d)), const((d, d)), const((1, d))]
    args = [oa, ob, proj, proj, x, w_a, w_b, w_o, norm_w.astype(F32).reshape(1, d)]
    out_specs = [row(d), row(d)]
    out_shape = [jax.ShapeDtypeStruct((t, d), F32),
                 jax.ShapeDtypeStruct((t, d), F32 if router is not None else BF16)]
    if router is not None:
        rw, rb = router
        rw_pad = jnp.zeros((d, HEAD_W), F32).at[:, :N_EXPERTS].set(rw.astype(F32))
        rw_hi = rw_pad.astype(BF16)
        rw_lo = (rw_pad - rw_hi.astype(F32)).astype(BF16)
        rb_pad = jnp.full((1, HEAD_W), -jnp.inf, F32).at[0, :N_EXPERTS].set(rb.astype(F32))
        in_specs += [const((d, HEAD_W)), const((d, HEAD_W)), const((1, HEAD_W))]
        args += [rw_hi, rw_lo, rb_pad]
        out_specs.append(row(HEAD_W))
        out_shape.append(jax.ShapeDtypeStruct((t, HEAD_W), F32))
    return pl.pallas_call(
        functools.partial(_mixer_out_body, with_router=router is not None),
        grid=(t // tm,),
        in_specs=in_specs,
        out_specs=out_specs,
        out_shape=out_shape,
        compiler_params=_cparams(("parallel",)),
        name="mixer_out",
    )(*args)


def _ffn_body(*refs, with_norm, tc):
    if with_norm:
        h_ref, x_ref, wgu_ref, wd_ref, nw_ref, xo_ref, ho_ref, acc_ref = refs
    else:
        h_ref, x_ref, wgu_ref, wd_ref, xo_ref, acc_ref = refs
    h = h_ref[...]
    d_ff = wd_ref.shape[0]
    for ci, c0 in enumerate(range(0, d_ff, tc)):
        g = jnp.dot(h, wgu_ref[:, c0:c0 + tc], preferred_element_type=F32)
        u = jnp.dot(h, wgu_ref[:, d_ff + c0:d_ff + c0 + tc], preferred_element_type=F32)
        a = (g * jax.nn.sigmoid(g) * u).astype(BF16)
        part = jnp.dot(a, wd_ref[c0:c0 + tc, :], preferred_element_type=F32)
        if ci == 0:
            acc_ref[...] = x_ref[...] + part
        else:
            acc_ref[...] += part
    x_new = acc_ref[...]
    xo_ref[...] = x_new
    if with_norm:
        ho_ref[...] = _rms(x_new, nw_ref[...]).astype(ho_ref.dtype)


def dense_ffn(h, x, w_gu, w_d, next_norm_w=None, tm=512, tc=256):
    t, d = x.shape
    d_ff = w_d.shape[0]
    row = pl.BlockSpec((tm, d), lambda i: (i, 0))
    const = lambda shape: pl.BlockSpec(shape, lambda i: (0, 0), pipeline_mode=pl.Buffered(1))
    in_specs = [row, row, const((d, 2 * d_ff)), const((d_ff, d))]
    args = [h, x, w_gu, w_d]
    out_specs = [row]
    out_shape = [jax.ShapeDtypeStruct((t, d), F32)]
    if next_norm_w is not None:
        in_specs.append(pl.BlockSpec((1, d), lambda i: (0, 0)))
        args.append(next_norm_w.astype(F32).reshape(1, d))
        out_specs.append(row)
        out_shape.append(jax.ShapeDtypeStruct((t, d), BF16))
    return pl.pallas_call(
        functools.partial(_ffn_body, with_norm=next_norm_w is not None, tc=tc),
        grid=(t // tm,),
        in_specs=in_specs,
        out_specs=out_specs,
        out_shape=out_shape,
        scratch_shapes=[pltpu.VMEM((tm, d), F32)],
        compiler_params=_cparams(("parallel",)),
        name="dense_ffn",
    )(*args)


def _expert_body(be_ref, tok_ref, nl_ref, h_ref, wg_ref, wu_ref, wd_ref, o_ref, xbuf_ref, xb_ref, acc_ref, sem,
                 *, n_c):
    del be_ref
    blk = pl.program_id(0)
    c = pl.program_id(1)
    n_blk = pl.num_programs(0)
    tm = xb_ref.shape[0]
    slot = blk % 2
    live = blk < nl_ref[0]
    per_step = tm // n_c

    def row_copy(b, grp, i, s):
        tok = tok_ref[b * tm + grp * SUBLANES + i]
        return pltpu.make_async_copy(h_ref.at[pl.ds(tok, 1)], xbuf_ref.at[s, grp, pl.ds(i, 1)], sem.at[s])

    def wait_block(s):
        def body(g, carry):
            for j in range(WAIT_RUN // SUBLANES):
                for i in range(SUBLANES):
                    dst = xbuf_ref.at[s, g * (WAIT_RUN // SUBLANES) + j, pl.ds(i, 1)]
                    pltpu.make_async_copy(h_ref.at[pl.ds(0, 1)], dst, sem.at[s]).wait()
            return carry
        lax.fori_loop(0, tm // WAIT_RUN, body, 0)

    @pl.when(jnp.logical_and(blk == 0, c == 0))
    def _():
        def body(grp, carry):
            for i in range(SUBLANES):
                row_copy(0, grp, i, 0).start()
            return carry
        lax.fori_loop(0, tm // SUBLANES, body, 0)

    @pl.when(c == 0)
    def _():
        wait_block(slot)
        xb_ref[...] = xbuf_ref[slot].reshape(tm, xb_ref.shape[1]).astype(BF16)
        acc_ref[...] = jnp.zeros(acc_ref.shape, F32)

    nxt = jnp.where(blk + 1 < n_blk, blk + 1, 0)

    def gather_next():
        for j in range(per_step // SUBLANES):
            for i in range(SUBLANES):
                row_copy(nxt, c * (per_step // SUBLANES) + j, i, 1 - slot).start()

    @pl.when(live)
    def _():
        gather_next()
        xb = xb_ref[...]
        g = jnp.dot(xb, wg_ref[0], preferred_element_type=F32)
        u = jnp.dot(xb, wu_ref[0], preferred_element_type=F32)
        a = (g * jax.nn.sigmoid(g) * u).astype(BF16)
        acc_ref[...] += jnp.dot(a, wd_ref[0], preferred_element_type=F32)

    @pl.when(jnp.logical_not(live))
    def _():
        gather_next()

    @pl.when(c == n_c - 1)
    def _():
        o_ref[...] = acc_ref[...]

    @pl.when(jnp.logical_and(blk == n_blk - 1, c == n_c - 1))
    def _():
        wait_block(1 - slot)


def moe_experts(h, slot_tok, block_e, n_live, w_gu, w_d, tm=MOE_TM, tc=MOE_TC):
    n_slots = slot_tok.shape[0]
    d = h.shape[1]
    d_e = w_d.shape[1]
    n_c = d_e // tc
    assert tm % (n_c * SUBLANES) == 0, "each grid step gathers an equal share of the next block's row groups"
    assert tm % WAIT_RUN == 0 and WAIT_RUN % SUBLANES == 0
    grid_spec = pltpu.PrefetchScalarGridSpec(
        num_scalar_prefetch=3,
        grid=(n_slots // tm, n_c),
        in_specs=[pl.BlockSpec(memory_space=pl.ANY),
                  pl.BlockSpec((1, d, tc), lambda b, c, be, tok, nl: (be[b], 0, c)),
                  pl.BlockSpec((1, d, tc), lambda b, c, be, tok, nl: (be[b], 0, n_c + c)),
                  pl.BlockSpec((1, tc, d), lambda b, c, be, tok, nl: (be[b], c, 0))],
        out_specs=pl.BlockSpec((tm, d), lambda b, c, be, tok, nl: (b, 0)),
        scratch_shapes=[pltpu.VMEM((2, tm // SUBLANES, SUBLANES, d), F32), pltpu.VMEM((tm, d), BF16),
                        pltpu.VMEM((tm, d), F32), pltpu.SemaphoreType.DMA((2,))],
    )
    return pl.pallas_call(
        functools.partial(_expert_body, n_c=n_c),
        grid_spec=grid_spec,
        out_shape=jax.ShapeDtypeStruct((n_slots, d), F32),
        compiler_params=_cparams(("arbitrary", "arbitrary")),
        name="moe_experts",
    )(block_e, slot_tok, n_live, h, w_gu, w_gu, w_d)


def _combine_body(dest_ref, x_ref, r_ref, yb_ref, o_ref, buf_ref, sem, *, tb):
    step = pl.program_id(0)
    slot = step % 2

    def issue(st, s):
        def body(g, carry):
            for i in range(SUBLANES):
                for k in range(2):
                    src = yb_ref.at[pl.ds(dest_ref[2 * (st * tb + g * SUBLANES + i) + k], 1)]
                    pltpu.make_async_copy(src, buf_ref.at[s, k, g, pl.ds(i, 1)], sem.at[s]).start(priority=k)
            return carry
        lax.fori_loop(0, tb // SUBLANES, body, 0)

    def wait_all(s):
        def body(g, carry):
            for j in range(WAIT_RUN // SUBLANES):
                for i in range(SUBLANES):
                    for k in range(2):
                        dst = buf_ref.at[s, k, g * (WAIT_RUN // SUBLANES) + j, pl.ds(i, 1)]
                        pltpu.make_async_copy(yb_ref.at[pl.ds(0, 1)], dst, sem.at[s]).wait()
            return carry
        lax.fori_loop(0, tb // WAIT_RUN, body, 0)

    @pl.when(step == 0)
    def _():
        issue(0, 0)

    @pl.when(step + 1 < pl.num_programs(0))
    def _():
        issue(step + 1, 1 - slot)

    wait_all(slot)
    route = r_ref[...]
    y0 = buf_ref[slot, 0].reshape(o_ref.shape)
    y1 = buf_ref[slot, 1].reshape(o_ref.shape)
    o_ref[...] = x_ref[...] + route[:, 2:3] * y0 + route[:, 3:4] * y1


def moe_combine(x, route, yb, dest, tb=256):
    t, d = x.shape
    assert tb % WAIT_RUN == 0
    grid_spec = pltpu.PrefetchScalarGridSpec(
        num_scalar_prefetch=1,
        grid=(t // tb,),
        in_specs=[pl.BlockSpec((tb, d), lambda i, dest: (i, 0)),
                  pl.BlockSpec((tb, HEAD_W), lambda i, dest: (i, 0)),
                  pl.BlockSpec(memory_space=pl.ANY)],
        out_specs=pl.BlockSpec((tb, d), lambda i, dest: (i, 0)),
        scratch_shapes=[pltpu.VMEM((2, 2, tb // SUBLANES, SUBLANES, d), F32), pltpu.SemaphoreType.DMA((2,))],
    )
    return pl.pallas_call(
        functools.partial(_combine_body, tb=tb),
        grid_spec=grid_spec,
        out_shape=jax.ShapeDtypeStruct((t, d), F32),
        compiler_params=_cparams(("arbitrary",)),
        name="moe_combine",
    )(dest, x, route, yb)


def moe_layer(hn, x, route, w_gu, w_d, tm=MOE_TM):
    t, d = x.shape
    n_assign = 2 * t
    e_flat = route[:, :2].astype(jnp.int32).reshape(n_assign)
    onehot = (e_flat[:, None] == jnp.arange(N_EXPERTS, dtype=jnp.int32)[None, :]).astype(jnp.int32)
    incl = jnp.cumsum(onehot, axis=0)
    rank = jnp.sum((incl - onehot) * onehot, axis=1)
    counts = incl[-1]
    padded = (counts + tm - 1) // tm * tm
    pad_end = jnp.cumsum(padded)
    pad_start = pad_end - padded
    dest = (pad_start[e_flat] + rank).astype(jnp.int32)
    n_blocks = -(-n_assign // tm) + N_EXPERTS
    block_start = jnp.arange(n_blocks, dtype=jnp.int32) * tm
    block_e = jnp.minimum(jnp.sum((pad_end[None, :] <= block_start[:, None]).astype(jnp.int32), axis=1),
                          N_EXPERTS - 1).astype(jnp.int32)
    slot_tok = jnp.zeros((n_blocks * tm,), jnp.int32).at[dest].set(jnp.arange(n_assign, dtype=jnp.int32) // 2,
                                                                  unique_indices=True)
    n_live = (pad_end[-1:] // tm).astype(jnp.int32)
    yb = moe_experts(hn, slot_tok, block_e, n_live, w_gu, w_d)
    return moe_combine(x, route, yb, dest)


def _rope_coefficients(positions):
    half = ROT_DIM // 2
    lane = jnp.arange(HEAD_W) % ATT_DIM
    inv_freq = ROPE_THETA ** (-(2.0 * (lane % half)).astype(F32) / ROT_DIM)
    freq = jnp.where(lane < ROT_DIM, inv_freq, 0.0)
    ang = positions.astype(F32).reshape(-1, 1) * freq[None, :]
    sin = jnp.sin(ang)
    return (jnp.cos(ang), jnp.where(lane < half, -sin, 0.0),
            jnp.where((lane >= half) & (lane < ROT_DIM), sin, 0.0))


def _split_w_in(w_in):
    att = 3 * GROUP_W
    conv = 3 * GROUP_W
    o_z = att + conv
    o_small = o_z + GROUP_W
    o_ga = o_small + 2 * GDN_HEADS
    d = w_in.shape[0]
    main = jnp.concatenate([w_in[:, o_ga:], w_in[:, :o_small]], axis=1).astype(BF16)
    small = jnp.zeros((d, HEAD_W), F32).at[:, :2 * GDN_HEADS].set(w_in[:, o_small:o_ga]).astype(BF16)
    return main, small


def kernel(x, positions, norm_mix_w, w_in, q_norm_w, k_norm_w, lam_vec, subln_w, conv_w, a_log, dt_bias,
           gdn_norm_w, w_branch_a, w_branch_b, w_out, norm_ffn_w, ffn_w_gate_up, ffn_w_down, router_w,
           router_b, moe_w_gate_up, moe_w_down):
    batch, seq, d = x.shape
    depth = w_in.shape[0]
    t = batch * seq
    xt = x.reshape(t, d).astype(F32)
    rope_c, rope_s1, rope_s2 = _rope_coefficients(positions)
    h = None
    for layer in range(depth):
        lam_init = 0.8 - 0.6 * math.exp(-0.3 * layer)
        is_moe = layer % 2 == 1
        j = layer // 2
        w_main, w_small = _split_w_in(w_in[layer])
        rope = (rope_c, rope_s1, rope_s2)
        if h is None:
            proj, small, qs, kk, vx = in_proj(xt, w_main, w_small, rope, q_norm_w[layer], k_norm_w[layer],
                                              norm_mix_w[layer])
        else:
            proj, small, qs, kk, vx = in_proj(h, w_main, w_small, rope, q_norm_w[layer], k_norm_w[layer])
        oa = flash_diff_attention(qs, kk, vx, q_norm_w[layer], k_norm_w[layer], lam_vec[layer], subln_w[layer],
                                  lam_init, batch, seq)
        ob = gated_delta_net(proj, small, conv_w[layer], a_log[layer], dt_bias[layer], gdn_norm_w[layer], batch, seq)
        router = (router_w[j], router_b[j]) if is_moe else None
        outs = mixer_out(oa, ob, proj, xt, w_branch_a[layer].astype(BF16), w_branch_b[layer].astype(BF16),
                         w_out[layer].astype(BF16), norm_ffn_w[layer], router)
        if is_moe:
            xt, hn, route = outs
            xt = moe_layer(hn, xt, route, moe_w_gate_up[j].astype(BF16), moe_w_down[j].astype(BF16))
            h = None
        else:
            xt, hn = outs
            nxt = norm_mix_w[layer + 1] if layer + 1 < depth else None
            res = dense_ffn(hn, xt, ffn_w_gate_up[j].astype(BF16), ffn_w_down[j].astype(BF16), nxt)
            if nxt is not None:
                xt, h = res
            else:
                xt = res[0]
    return xt.reshape(batch, seq, d)
```
